```python
import math
import jax, jax.numpy as jnp
from jax import lax
import numpy as np

D_MODEL = 1024
BATCH = 8
SEQ = 4096
DEPTH = 1

MEM_LEN = 256
DIFF_HEADS = 8
DIFF_HEAD_DIM = 64
DIFF_V_DIM = 2 * DIFF_HEAD_DIM
DIFF_QK_WIDTH = DIFF_HEADS * 2 * DIFF_HEAD_DIM
DIFF_WIDTH = DIFF_HEADS * DIFF_V_DIM
CONV_WIDTH = D_MODEL
CONV_K = 3
MEM_HEADS = 4
MEM_HEAD_DIM = 256
MEM_WIDTH = MEM_HEADS * MEM_HEAD_DIM
N_BRANCH = 3
D_FF = 4 * D_MODEL
Q_BLOCK = 128
NORM_EPS = 1e-6
MASK_VALUE = -1e30

SPLIT_SIZES = (DIFF_QK_WIDTH, DIFF_QK_WIDTH, DIFF_WIDTH, 3 * CONV_WIDTH, MEM_WIDTH, N_BRANCH * D_MODEL)
SPLIT_POINTS = tuple(int(v) for v in np.cumsum(SPLIT_SIZES)[:-1])
PROJ_WIDTH = int(sum(SPLIT_SIZES))

kernel_name = "hybrid_diffattn_shortconv_memxattn_block"


def rms_norm(x, g):
    xf = x.astype(jnp.float32)
    y = xf * lax.rsqrt(jnp.mean(xf * xf, axis=-1, keepdims=True) + NORM_EPS)
    return (y * g.astype(jnp.float32)).astype(x.dtype)


def diff_attention(q_d, k_d, v_d, lam, q_norm_g, k_norm_g):
    B, S = q_d.shape[0], q_d.shape[1]
    q = rms_norm(q_d.reshape(B, S, DIFF_HEADS, 2, DIFF_HEAD_DIM), q_norm_g)
    q = q * jnp.asarray(DIFF_HEAD_DIM ** -0.5, q.dtype)
    k = rms_norm(k_d.reshape(B, S, DIFF_HEADS, 2, DIFF_HEAD_DIM), k_norm_g)
    v = v_d.reshape(B, S, DIFF_HEADS, DIFF_V_DIM)
    nb = S // Q_BLOCK
    q_blocks = jnp.moveaxis(q.reshape(B, nb, Q_BLOCK, DIFF_HEADS, 2, DIFF_HEAD_DIM), 1, 0)
    key_pos = jnp.arange(S)

    def one_block(args):
        q_blk, blk = args
        s = jnp.einsum('bqhcd,bkhcd->bhcqk', q_blk, k).astype(jnp.float32)
        q_pos = blk * Q_BLOCK + jnp.arange(Q_BLOCK)
        causal = key_pos[None, :] <= q_pos[:, None]
        s = jnp.where(causal, s, MASK_VALUE)
        p = jax.nn.softmax(s, axis=-1)
        a = p[:, :, 0] - lam * p[:, :, 1]
        return jnp.einsum('bhqk,bkhe->bqhe', a.astype(v.dtype), v)

    o = lax.map(one_block, (q_blocks, jnp.arange(nb)))
    return jnp.moveaxis(o, 0, 1).reshape(B, S, DIFF_HEADS, DIFF_V_DIM)


def short_gated_conv(conv_in, conv_w):
    x_c, gate_b, gate_c = jnp.split(conv_in, 3, axis=-1)
    inner = gate_c * x_c
    y = lax.conv_general_dilated(
        inner, conv_w[:, None, :], window_strides=(1,), padding=[(CONV_K - 1, 0)],
        dimension_numbers=('NWC', 'WIO', 'NWC'), feature_group_count=CONV_WIDTH)
    return gate_b * y


def memory_attention(q_m, mem_n, w_mem_kv, mq_norm_g, mk_norm_g):
    B, S = q_m.shape[0], q_m.shape[1]
    M = mem_n.shape[1]
    kv = mem_n @ w_mem_kv
    k_m, v_m = jnp.split(kv, 2, axis=-1)
    q = rms_norm(q_m.reshape(B, S, MEM_HEADS, MEM_HEAD_DIM), mq_norm_g)
    q = q * jnp.asarray(MEM_HEAD_DIM ** -0.5, q.dtype)
    k = rms_norm(k_m.reshape(B, M, MEM_HEADS, MEM_HEAD_DIM), mk_norm_g)
    v = v_m.reshape(B, M, MEM_HEADS, MEM_HEAD_DIM)
    s = jnp.einsum('bshd,bmhd->bhsm', q, k).astype(jnp.float32)
    p = jax.nn.softmax(s, axis=-1)
    o = jnp.einsum('bhsm,bmhd->bshd', p.astype(v.dtype), v)
    return o.reshape(B, S, MEM_WIDTH)


def hybrid_layer(x, mem, lam_init, norm_mix_g, norm_mem_g, w_in, b_gate, q_norm_g, k_norm_g,
                 lam_q1, lam_k1, lam_q2, lam_k2, subln_g, w_attn_o, conv_w, w_conv_o,
                 w_mem_kv, mq_norm_g, mk_norm_g, w_mem_o, w_o, norm_mlp_g, w_mlp_in, w_mlp_out):
    B, S, D = x.shape
    h = rms_norm(x, norm_mix_g)
    proj = h @ w_in
    q_d, k_d, v_d, conv_in, q_m, gate_logits = jnp.split(proj, SPLIT_POINTS, axis=-1)

    lam = (jnp.exp(jnp.sum(lam_q1.astype(jnp.float32) * lam_k1.astype(jnp.float32)))
           - jnp.exp(jnp.sum(lam_q2.astype(jnp.float32) * lam_k2.astype(jnp.float32)))
           + lam_init)
    o = diff_attention(q_d.reshape(B, S, DIFF_HEADS, -1), k_d.reshape(B, S, DIFF_HEADS, -1),
                       v_d.reshape(B, S, DIFF_HEADS, -1), lam, q_norm_g, k_norm_g)
    o = rms_norm(o, subln_g) * jnp.asarray(1.0 - lam_init, o.dtype)
    y_attn = o.reshape(B, S, DIFF_WIDTH) @ w_attn_o

    y_conv = short_gated_conv(conv_in, conv_w) @ w_conv_o

    mem_n = rms_norm(mem, norm_mem_g)
    y_mem = memory_attention(q_m, mem_n, w_mem_kv, mq_norm_g, mk_norm_g) @ w_mem_o

    g = jax.nn.sigmoid((gate_logits + b_gate).astype(jnp.float32)).astype(x.dtype)
    g = g.reshape(B, S, N_BRANCH, D)
    merged = g[:, :, 0] * y_attn + g[:, :, 1] * y_conv + g[:, :, 2] * y_mem
    x = x + merged @ w_o

    h2 = rms_norm(x, norm_mlp_g)
    x = x + jnp.square(jax.nn.relu(h2 @ w_mlp_in)) @ w_mlp_out
    return x


def setup_inputs(seed: int = 0) -> dict:
    key = jax.random.key(seed)
    ks = jax.random.split(key, 26)
    f32 = jnp.float32
    L = DEPTH

    def nrm(k, shape, scale):
        return jax.random.normal(k, shape, f32) * scale

    def gain(k, shape):
        return 1.0 + 0.05 * jax.random.normal(k, shape, f32)

    return {
        'x': nrm(ks[0], (BATCH, SEQ, D_MODEL), 1.0),
        'mem': nrm(ks[1], (BATCH, MEM_LEN, D_MODEL), 1.0),
        'norm_mix_g': gain(ks[2], (L, D_MODEL)),
        'norm_mem_g': gain(ks[3], (L, D_MODEL)),
        'w_in': nrm(ks[4], (L, D_MODEL, PROJ_WIDTH), D_MODEL ** -0.5),
        'b_gate': nrm(ks[5], (L, N_BRANCH * D_MODEL), 0.1),
        'q_norm_g': gain(ks[6], (L, DIFF_HEAD_DIM)),
        'k_norm_g': gain(ks[7], (L, DIFF_HEAD_DIM)),
        'lam_q1': nrm(ks[8], (L, DIFF_HEAD_DIM), 0.1),
        'lam_k1': nrm(ks[9], (L, DIFF_HEAD_DIM), 0.1),
        'lam_q2': nrm(ks[10], (L, DIFF_HEAD_DIM), 0.1),
        'lam_k2': nrm(ks[11], (L, DIFF_HEAD_DIM), 0.1),
        'subln_g': gain(ks[12], (L, DIFF_V_DIM)),
        'w_attn_o': nrm(ks[13], (L, DIFF_WIDTH, D_MODEL), DIFF_WIDTH ** -0.5),
        'conv_w': nrm(ks[14], (L, CONV_K, CONV_WIDTH), CONV_K ** -0.5),
        'w_conv_o': nrm(ks[15], (L, CONV_WIDTH, D_MODEL), CONV_WIDTH ** -0.5),
        'w_mem_kv': nrm(ks[16], (L, D_MODEL, 2 * MEM_WIDTH), D_MODEL ** -0.5),
        'mq_norm_g': gain(ks[17], (L, MEM_HEAD_DIM)),
        'mk_norm_g': gain(ks[18], (L, MEM_HEAD_DIM)),
        'w_mem_o': nrm(ks[19], (L, MEM_WIDTH, D_MODEL), MEM_WIDTH ** -0.5),
        'w_o': nrm(ks[20], (L, D_MODEL, D_MODEL), D_MODEL ** -0.5),
        'norm_mlp_g': gain(ks[21], (L, D_MODEL)),
        'w_mlp_in': nrm(ks[22], (L, D_MODEL, D_FF), D_MODEL ** -0.5),
        'w_mlp_out': nrm(ks[23], (L, D_FF, D_MODEL), D_FF ** -0.5),
    }


def reference(x, mem, norm_mix_g, norm_mem_g, w_in, b_gate, q_norm_g, k_norm_g,
              lam_q1, lam_k1, lam_q2, lam_k2, subln_g, w_attn_o, conv_w, w_conv_o,
              w_mem_kv, mq_norm_g, mk_norm_g, w_mem_o, w_o, norm_mlp_g, w_mlp_in, w_mlp_out):
    for l in range(DEPTH):
        lam_init = 0.8 - 0.6 * math.exp(-0.3 * l)
        x = hybrid_layer(x, mem, lam_init, norm_mix_g[l], norm_mem_g[l], w_in[l], b_gate[l],
                         q_norm_g[l], k_norm_g[l], lam_q1[l], lam_k1[l], lam_q2[l], lam_k2[l],
                         subln_g[l], w_attn_o[l], conv_w[l], w_conv_o[l], w_mem_kv[l],
                         mq_norm_g[l], mk_norm_g[l], w_mem_o[l], w_o[l], norm_mlp_g[l],
                         w_mlp_in[l], w_mlp_out[l])
    return x
```

```python
import functools

import jax
import jax.numpy as jnp
import numpy as np
from jax import lax
from jax.experimental import pallas as pl
from jax.experimental.pallas import tpu as pltpu

F32 = jnp.float32
BF16 = jnp.bfloat16

NORM_EPS = 1e-6
MASK_VALUE = -1e30

DIFF_HEADS = 8
DIFF_HEAD_DIM = 64
DIFF_V_DIM = 2 * DIFF_HEAD_DIM
MEM_HEADS = 4
MEM_HEAD_DIM = 256
CONV_K = 3

MXU_EDGE = 256
HALO = 16

COL_Q, COL_K, COL_V, COL_XC, COL_GB, COL_GC, COL_QM, COL_GATE = 0, 1, 2, 3, 4, 5, 6, 7
N_COL_BLOCKS = 10

VMEM_LIMIT = 56 * 1024 * 1024


def _rms(x, g):
    ms = jnp.mean(x * x, axis=-1, keepdims=True)
    return x * lax.rsqrt(ms + NORM_EPS) * g


def _dot(a, b):
    return jnp.dot(a, b, preferred_element_type=F32)


def _dot_nt(a, b):
    return lax.dot_general(a, b, (((1,), (1,)), ((), ())), preferred_element_type=F32)


def _mem_kv_kernel(mem_ref, g_ref, w_ref, kg_ref, kt_ref, v_ref):
    mem_n = _rms(mem_ref[0], g_ref[...]).astype(BF16)
    kv = _dot(mem_n, w_ref[...])
    width = kt_ref.shape[1]
    for h in range(MEM_HEADS):
        sl = slice(h * MEM_HEAD_DIM, (h + 1) * MEM_HEAD_DIM)
        kn = _rms(kv[:, sl], kg_ref[...])
        kt_ref[0, sl, :] = kn.T.astype(BF16)
    v_ref[0] = kv[:, width:].astype(BF16)


def _mem_kv(mem, g, w_kv, kg):
    b, m, d = mem.shape
    width = w_kv.shape[1] // 2
    return pl.pallas_call(
        _mem_kv_kernel,
        grid=(b,),
        in_specs=[
            pl.BlockSpec((1, m, d), lambda i: (i, 0, 0)),
            pl.BlockSpec((1, d), lambda i: (0, 0)),
            pl.BlockSpec((d, 2 * width), lambda i: (0, 0)),
            pl.BlockSpec((1, MEM_HEAD_DIM), lambda i: (0, 0)),
        ],
        out_specs=[
            pl.BlockSpec((1, width, m), lambda i: (i, 0, 0)),
            pl.BlockSpec((1, m, width), lambda i: (i, 0, 0)),
        ],
        out_shape=[
            jax.ShapeDtypeStruct((b, width, m), BF16),
            jax.ShapeDtypeStruct((b, m, width), BF16),
        ],
        compiler_params=pltpu.CompilerParams(
            dimension_semantics=("parallel",), vmem_limit_bytes=VMEM_LIMIT),
        name="mem_kv",
    )(mem, g, w_kv, kg)


def _norm_slot(n):
    return jnp.where(n == COL_K, 1, jnp.where(n == COL_QM, 2, 0))


def _in_proj_kernel(x_ref, g_ref, w_ref, e_ref, gn_ref, o_ref, h_scr):
    n = pl.program_id(1)

    @pl.when(n == 0)
    def _():
        h_scr[...] = _rms(x_ref[...], g_ref[...]).astype(BF16)

    acc = _dot(h_scr[...], w_ref[...])
    normed = (n == COL_Q) | (n == COL_K) | (n == COL_QM)

    @pl.when(normed)
    def _():
        sq = (acc * acc).astype(BF16)
        for c in range(acc.shape[1] // MXU_EDGE):
            sl = slice(c * MXU_EDGE, (c + 1) * MXU_EDGE)
            ms = _dot(sq[:, sl], e_ref[0])
            y = acc[:, sl] * lax.rsqrt(ms + NORM_EPS) * gn_ref[0, :, sl]
            o_ref[:, sl] = y.astype(BF16)

    @pl.when(jnp.logical_not(normed))
    def _():
        o_ref[...] = acc.astype(BF16)


def _in_proj(x2d, g, w_in, e_mats, gn, tm):
    m, d = x2d.shape
    return pl.pallas_call(
        _in_proj_kernel,
        grid=(m // tm, N_COL_BLOCKS),
        in_specs=[
            pl.BlockSpec((tm, d), lambda i, n: (i, 0)),
            pl.BlockSpec((1, d), lambda i, n: (0, 0)),
            pl.BlockSpec((d, d), lambda i, n: (0, n)),
            pl.BlockSpec((1, MXU_EDGE, MXU_EDGE), lambda i, n: (_norm_slot(n), 0, 0)),
            pl.BlockSpec((1, 1, d), lambda i, n: (_norm_slot(n), 0, 0)),
        ],
        out_specs=pl.BlockSpec((tm, d), lambda i, n: (i, n)),
        out_shape=jax.ShapeDtypeStruct((m, N_COL_BLOCKS * d), BF16),
        scratch_shapes=[pltpu.VMEM((tm, d), BF16)],
        compiler_params=pltpu.CompilerParams(
            dimension_semantics=("parallel", "arbitrary"), vmem_limit_bytes=VMEM_LIMIT),
        name="in_proj",
    )(x2d, g, w_in, e_mats, gn)


def _diff_attn_kernel(lq1_ref, lk1_ref, lq2_ref, lk2_ref, sg_ref, q_ref, k_ref, v_ref, o_ref,
                      *, tq, lam_init):
    i = pl.program_id(2)
    q = q_ref[0]
    lane = lax.broadcasted_iota(jnp.int32, q.shape, 1)
    zero = jnp.zeros_like(q)
    qs = jnp.concatenate([jnp.where(lane < DIFF_HEAD_DIM, q, zero),
                          jnp.where(lane >= DIFF_HEAD_DIM, q, zero)], axis=0)

    def step(j, carry, masked):
        m, l, acc = carry
        start = pl.multiple_of(j * tq, tq)
        k = k_ref[0, pl.ds(start, tq), :]
        v = v_ref[0, pl.ds(start, tq), :]
        s = _dot_nt(qs, k)
        if masked:
            row = lax.broadcasted_iota(jnp.int32, (tq, tq), 0)
            col = lax.broadcasted_iota(jnp.int32, (tq, tq), 1)
            causal = jnp.concatenate([col <= row, col <= row], axis=0)
            s = jnp.where(causal, s, MASK_VALUE)
        m_new = jnp.maximum(m, jnp.max(s, axis=-1, keepdims=True))
        alpha = jnp.exp(m - m_new)
        p = jnp.exp(s - m_new)
        l = alpha * l + jnp.sum(p, axis=-1, keepdims=True)
        acc = alpha * acc + _dot(p.astype(BF16), v)
        return m_new, l, acc

    init = (jnp.full((2 * tq, 1), MASK_VALUE, F32), jnp.zeros((2 * tq, 1), F32),
            jnp.zeros((2 * tq, DIFF_V_DIM), F32))
    carry = lax.fori_loop(0, i, functools.partial(step, masked=False), init)
    _, l, acc = step(i, carry, masked=True)

    lam = (jnp.exp(jnp.sum(lq1_ref[...] * lk1_ref[...], axis=-1, keepdims=True))
           - jnp.exp(jnp.sum(lq2_ref[...] * lk2_ref[...], axis=-1, keepdims=True))
           + lam_init)
    o = acc[:tq] / l[:tq] - lam * (acc[tq:] / l[tq:])
    o_ref[0] = (_rms(o, sg_ref[...]) * (1.0 - lam_init)).astype(BF16)


def _diff_attn(proj3d, lq1, lk1, lq2, lk2, subln_g, lam_init, tq):
    b, s, _ = proj3d.shape
    vec = pl.BlockSpec((1, DIFF_HEAD_DIM), lambda bi, h, i: (0, 0))
    return pl.pallas_call(
        functools.partial(_diff_attn_kernel, tq=tq, lam_init=lam_init),
        grid=(b, DIFF_HEADS, s // tq),
        in_specs=[
            vec, vec, vec, vec,
            pl.BlockSpec((1, DIFF_V_DIM), lambda bi, h, i: (0, 0)),
            pl.BlockSpec((1, tq, DIFF_V_DIM), lambda bi, h, i: (bi, i, COL_Q * DIFF_HEADS + h)),
            pl.BlockSpec((1, s, DIFF_V_DIM), lambda bi, h, i: (bi, 0, COL_K * DIFF_HEADS + h)),
            pl.BlockSpec((1, s, DIFF_V_DIM), lambda bi, h, i: (bi, 0, COL_V * DIFF_HEADS + h)),
        ],
        out_specs=pl.BlockSpec((1, tq, DIFF_V_DIM), lambda bi, h, i: (bi, i, h)),
        out_shape=jax.ShapeDtypeStruct((b, s, DIFF_HEADS * DIFF_V_DIM), BF16),
        compiler_params=pltpu.CompilerParams(
            dimension_semantics=("parallel", "parallel", "parallel"),
            vmem_limit_bytes=VMEM_LIMIT),
        name="diff_attn",
    )(lq1, lk1, lq2, lk2, subln_g, proj3d, proj3d, proj3d)


def _merge_kernel(x_ref, on_ref, xc_ref, gb_ref, gc_ref, hxc_ref, hgc_ref, qm_ref,
                  la_ref, lc_ref, lm_ref, kt_ref, vm_ref, cw_ref, bg_ref,
                  wa_ref, wc_ref, wm_ref, wo_ref, o_ref, inner_scr, *, tiles_per_seq):
    i = pl.program_id(0)
    tm, d = x_ref.shape

    ya = _dot(on_ref[...], wa_ref[...])

    first = (i % tiles_per_seq) == 0
    halo = hgc_ref[...].astype(F32) * hxc_ref[...].astype(F32)
    inner_scr[0:HALO, :] = jnp.where(first, 0.0, halo)
    inner = gc_ref[...].astype(F32) * xc_ref[...].astype(F32)
    inner_scr[HALO:HALO + tm, :] = inner
    conv = (cw_ref[2:3, :] * inner
            + cw_ref[1:2, :] * inner_scr[HALO - 1:HALO - 1 + tm, :]
            + cw_ref[0:1, :] * inner_scr[HALO - 2:HALO - 2 + tm, :])
    yc = _dot((gb_ref[...].astype(F32) * conv).astype(BF16), wc_ref[...])

    heads = []
    for h in range(MEM_HEADS):
        sl = slice(h * MEM_HEAD_DIM, (h + 1) * MEM_HEAD_DIM)
        s = _dot(qm_ref[:, sl], kt_ref[0, sl, :])
        p = jnp.exp(s - jnp.max(s, axis=-1, keepdims=True))
        p = p / jnp.sum(p, axis=-1, keepdims=True)
        heads.append(_dot(p.astype(BF16), vm_ref[0, :, sl]).astype(BF16))
    ym = _dot(jnp.concatenate(heads, axis=-1), wm_ref[...])

    ga = jax.nn.sigmoid(la_ref[...].astype(F32) + bg_ref[:, 0:d])
    gc = jax.nn.sigmoid(lc_ref[...].astype(F32) + bg_ref[:, d:2 * d])
    gm = jax.nn.sigmoid(lm_ref[...].astype(F32) + bg_ref[:, 2 * d:3 * d])
    merged = ga * ya + gc * yc + gm * ym
    o_ref[...] = x_ref[...] + _dot(merged.astype(BF16), wo_ref[...])


def _merge(x2d, on2d, proj, kt, vm, conv_w, b_gate, wa, wc, wm, wo, tm, seq):
    m, d = x2d.shape
    tiles_per_seq = seq // tm
    halo_blocks = tm // HALO

    def col(c):
        return pl.BlockSpec((tm, d), lambda i: (i, c))

    def halo(c):
        return pl.BlockSpec((HALO, d), lambda i: (jnp.maximum(i * halo_blocks - 1, 0), c))

    def whole(shape):
        return pl.BlockSpec(shape, lambda i: (0,) * len(shape))

    mem_len = vm.shape[1]
    return pl.pallas_call(
        functools.partial(_merge_kernel, tiles_per_seq=tiles_per_seq),
        grid=(m // tm,),
        in_specs=[
            pl.BlockSpec((tm, d), lambda i: (i, 0)),
            pl.BlockSpec((tm, d), lambda i: (i, 0)),
            col(COL_XC), col(COL_GB), col(COL_GC), halo(COL_XC), halo(COL_GC), col(COL_QM),
            col(COL_GATE), col(COL_GATE + 1), col(COL_GATE + 2),
            pl.BlockSpec((1, d, mem_len), lambda i: (i // tiles_per_seq, 0, 0)),
            pl.BlockSpec((1, mem_len, d), lambda i: (i // tiles_per_seq, 0, 0)),
            whole((CONV_K, d)), whole((1, 3 * d)),
            whole((d, d)), whole((d, d)), whole((d, d)), whole((d, d)),
        ],
        out_specs=pl.BlockSpec((tm, d), lambda i: (i, 0)),
        out_shape=jax.ShapeDtypeStruct((m, d), F32),
        scratch_shapes=[pltpu.VMEM((HALO + tm, d), F32)],
        compiler_params=pltpu.CompilerParams(
            dimension_semantics=("parallel",), vmem_limit_bytes=VMEM_LIMIT),
        name="merge",
    )(x2d, on2d, proj, proj, proj, proj, proj, proj, proj, proj, proj,
      kt, vm, conv_w, b_gate, wa, wc, wm, wo)


def _mlp_kernel(x_ref, g_ref, w1_ref, w2_ref, o_ref, h_scr):
    f = pl.program_id(1)

    @pl.when(f == 0)
    def _():
        x = x_ref[...]
        h_scr[...] = _rms(x, g_ref[...]).astype(BF16)
        o_ref[...] = x

    u = jnp.square(jnp.maximum(_dot(h_scr[...], w1_ref[...]), 0.0))
    o_ref[...] += _dot(u.astype(BF16), w2_ref[...])


def _mlp(x2d, g, w1, w2, tm, tf):
    m, d = x2d.shape
    dff = w1.shape[1]
    return pl.pallas_call(
        _mlp_kernel,
        grid=(m // tm, dff // tf),
        in_specs=[
            pl.BlockSpec((tm, d), lambda i, f: (i, 0)),
            pl.BlockSpec((1, d), lambda i, f: (0, 0)),
            pl.BlockSpec((d, tf), lambda i, f: (0, f)),
            pl.BlockSpec((tf, d), lambda i, f: (f, 0)),
        ],
        out_specs=pl.BlockSpec((tm, d), lambda i, f: (i, 0)),
        out_shape=jax.ShapeDtypeStruct((m, d), F32),
        scratch_shapes=[pltpu.VMEM((tm, d), BF16)],
        compiler_params=pltpu.CompilerParams(
            dimension_semantics=("parallel", "arbitrary"), vmem_limit_bytes=VMEM_LIMIT),
        name="mlp",
    )(x2d, g, w1, w2)


def _group_mean_matrix(group):
    idx = np.arange(MXU_EDGE) // group
    return (idx[:, None] == idx[None, :]).astype(np.float32) / group


def _layer(x, mem, lam_init, norm_mix_g, norm_mem_g, w_in, b_gate, q_norm_g, k_norm_g,
           lam_q1, lam_k1, lam_q2, lam_k2, subln_g, w_attn_o, conv_w, w_conv_o, w_mem_kv,
           mq_norm_g, mk_norm_g, w_mem_o, w_o, norm_mlp_g, w_mlp_in, w_mlp_out):
    b, s, d = x.shape
    x2d = x.reshape(b * s, d)
    row = lambda v: v.reshape(1, -1)

    e_mats = jnp.asarray(np.stack([_group_mean_matrix(DIFF_HEAD_DIM),
                                   _group_mean_matrix(DIFF_HEAD_DIM),
                                   _group_mean_matrix(MEM_HEAD_DIM)]), BF16)
    gn = jnp.stack([
        jnp.tile(q_norm_g * DIFF_HEAD_DIM ** -0.5, d // DIFF_HEAD_DIM),
        jnp.tile(k_norm_g, d // DIFF_HEAD_DIM),
        jnp.tile(mq_norm_g * MEM_HEAD_DIM ** -0.5, d // MEM_HEAD_DIM),
    ]).reshape(3, 1, d)

    kt, vm = _mem_kv(mem, row(norm_mem_g), w_mem_kv.astype(BF16), row(mk_norm_g))
    proj = _in_proj(x2d, row(norm_mix_g), w_in.astype(BF16), e_mats, gn, tm=512)
    o_n = _diff_attn(proj.reshape(b, s, -1), row(lam_q1), row(lam_k1), row(lam_q2), row(lam_k2),
                     row(subln_g), lam_init, tq=512)
    x1 = _merge(x2d, o_n.reshape(b * s, d), proj, kt, vm, conv_w, row(b_gate),
                w_attn_o.astype(BF16), w_conv_o.astype(BF16), w_mem_o.astype(BF16),
                w_o.astype(BF16), tm=256, seq=s)
    x2 = _mlp(x1, row(norm_mlp_g), w_mlp_in.astype(BF16), w_mlp_out.astype(BF16), tm=512, tf=1024)
    return x2.reshape(b, s, d)


def kernel(x, mem, norm_mix_g, norm_mem_g, w_in, b_gate, q_norm_g, k_norm_g, lam_q1, lam_k1,
           lam_q2, lam_k2, subln_g, w_attn_o, conv_w, w_conv_o, w_mem_kv, mq_norm_g, mk_norm_g,
           w_mem_o, w_o, norm_mlp_g, w_mlp_in, w_mlp_out):
    depth = w_in.shape[0]
    for layer in range(depth):
        lam_init = 0.8 - 0.6 * float(np.exp(-0.3 * layer))
        x = _layer(x, mem, lam_init, norm_mix_g[layer], norm_mem_g[layer], w_in[layer],
                   b_gate[layer], q_norm_g[layer], k_norm_g[layer], lam_q1[layer], lam_k1[layer],
                   lam_q2[layer], lam_k2[layer], subln_g[layer], w_attn_o[layer], conv_w[layer],
                   w_conv_o[layer], w_mem_kv[layer], mq_norm_g[layer], mk_norm_g[layer],
                   w_mem_o[layer], w_o[layer], norm_mlp_g[layer], w_mlp_in[layer],
                   w_mlp_out[layer])
    return x
```

```python
import functools

import jax
import jax.numpy as jnp
import numpy as np
from jax import lax
from jax.experimental import pallas as pl
from jax.experimental.pallas import tpu as pltpu

F32 = jnp.float32
BF16 = jnp.bfloat16

NORM_EPS = 1e-6
MASK_VALUE = -1e30
LOG2_E = 1.4426950408889634
MAX_UNSHIFTED_LOGIT = 60.0

DIFF_HEADS = 8
DIFF_HEAD_DIM = 64
DIFF_V_DIM = 2 * DIFF_HEAD_DIM
MEM_HEADS = 4
MEM_HEAD_DIM = 256
CONV_K = 3

MXU_EDGE = 256
HALO = 16

COL_Q, COL_K, COL_V, COL_XC, COL_GB, COL_GC, COL_QM, COL_GATE = 0, 1, 2, 3, 4, 5, 6, 7
N_COL_BLOCKS = 10

VMEM_LIMIT = 56 * 1024 * 1024


def _rms(x, g):
    ms = jnp.mean(x * x, axis=-1, keepdims=True)
    return x * lax.rsqrt(ms + NORM_EPS) * g


def _dot(a, b):
    return jnp.dot(a, b, preferred_element_type=F32)


def _dot_nt(a, b):
    return lax.dot_general(a, b, (((1,), (1,)), ((), ())), preferred_element_type=F32)


def _mem_kv_kernel(mem_ref, g_ref, w_ref, kg_ref, kt_ref, v_ref):
    mem_n = _rms(mem_ref[0], g_ref[...]).astype(BF16)
    kv = _dot(mem_n, w_ref[...])
    width = kt_ref.shape[1]
    for h in range(MEM_HEADS):
        sl = slice(h * MEM_HEAD_DIM, (h + 1) * MEM_HEAD_DIM)
        kn = _rms(kv[:, sl], kg_ref[...])
        kt_ref[0, sl, :] = kn.T.astype(BF16)
    v_ref[0] = kv[:, width:].astype(BF16)


def _mem_kv(mem, g, w_kv, kg):
    b, m, d = mem.shape
    width = w_kv.shape[1] // 2
    return pl.pallas_call(
        _mem_kv_kernel,
        grid=(b,),
        in_specs=[
            pl.BlockSpec((1, m, d), lambda i: (i, 0, 0)),
            pl.BlockSpec((1, d), lambda i: (0, 0)),
            pl.BlockSpec((d, 2 * width), lambda i: (0, 0)),
            pl.BlockSpec((1, MEM_HEAD_DIM), lambda i: (0, 0)),
        ],
        out_specs=[
            pl.BlockSpec((1, width, m), lambda i: (i, 0, 0)),
            pl.BlockSpec((1, m, width), lambda i: (i, 0, 0)),
        ],
        out_shape=[
            jax.ShapeDtypeStruct((b, width, m), BF16),
            jax.ShapeDtypeStruct((b, m, width), BF16),
        ],
        compiler_params=pltpu.CompilerParams(
            dimension_semantics=("parallel",), vmem_limit_bytes=VMEM_LIMIT),
        name="mem_kv",
    )(mem, g, w_kv, kg)


def _norm_slot(n):
    return jnp.where(n == COL_K, 1, jnp.where(n == COL_QM, 2, 0))


def _in_proj_kernel(x_ref, g_ref, w_ref, e_ref, gn_ref, o_ref, h_scr):
    n = pl.program_id(1)

    @pl.when(n == 0)
    def _():
        h_scr[...] = _rms(x_ref[...], g_ref[...]).astype(BF16)

    acc = _dot(h_scr[...], w_ref[...])
    normed = (n == COL_Q) | (n == COL_K) | (n == COL_QM)

    @pl.when(normed)
    def _():
        sq = (acc * acc).astype(BF16)
        for c in range(acc.shape[1] // MXU_EDGE):
            sl = slice(c * MXU_EDGE, (c + 1) * MXU_EDGE)
            ms = _dot(sq[:, sl], e_ref[0])
            y = acc[:, sl] * lax.rsqrt(ms + NORM_EPS) * gn_ref[0, :, sl]
            o_ref[:, sl] = y.astype(BF16)

    @pl.when(jnp.logical_not(normed))
    def _():
        o_ref[...] = acc.astype(BF16)


def _in_proj(x2d, g, w_in, e_mats, gn, tm):
    m, d = x2d.shape
    return pl.pallas_call(
        _in_proj_kernel,
        grid=(m // tm, N_COL_BLOCKS),
        in_specs=[
            pl.BlockSpec((tm, d), lambda i, n: (i, 0)),
            pl.BlockSpec((1, d), lambda i, n: (0, 0)),
            pl.BlockSpec((d, d), lambda i, n: (0, n)),
            pl.BlockSpec((1, MXU_EDGE, MXU_EDGE), lambda i, n: (_norm_slot(n), 0, 0)),
            pl.BlockSpec((1, 1, d), lambda i, n: (_norm_slot(n), 0, 0)),
        ],
        out_specs=pl.BlockSpec((tm, d), lambda i, n: (i, n)),
        out_shape=jax.ShapeDtypeStruct((m, N_COL_BLOCKS * d), BF16),
        scratch_shapes=[pltpu.VMEM((tm, d), BF16)],
        compiler_params=pltpu.CompilerParams(
            dimension_semantics=("parallel", "arbitrary"), vmem_limit_bytes=VMEM_LIMIT),
        name="in_proj",
    )(x2d, g, w_in, e_mats, gn)


def _diff_attn_kernel(lq1_ref, lk1_ref, lq2_ref, lk2_ref, sg_ref, q_ref, k_ref, v_ref, o_ref,
                      *scratch, tq, lam_init, online):
    i = pl.program_id(2)
    q = q_ref[0]
    lane = lax.broadcasted_iota(jnp.int32, q.shape, 1)
    zero = jnp.zeros_like(q)
    qs = jnp.concatenate([jnp.where(lane < DIFF_HEAD_DIM, q, zero),
                          jnp.where(lane >= DIFF_HEAD_DIM, q, zero)], axis=0)

    def kv(start, size):
        start = pl.multiple_of(start, size)
        return k_ref[0, pl.ds(start, size), :], v_ref[0, pl.ds(start, size), :]

    def causal_mask(s, n):
        row = lax.broadcasted_iota(jnp.int32, (n, n), 0)
        col = lax.broadcasted_iota(jnp.int32, (n, n), 1)
        keep = jnp.concatenate([col <= row] * (s.shape[0] // n), axis=0)
        return jnp.where(keep, s, MASK_VALUE)

    if online:
        def online_step(j, carry, masked):
            m, l, acc = carry
            k, v = kv(j * tq, tq)
            s = _dot_nt(qs, k)
            if masked:
                s = causal_mask(s, tq)
            m_new = jnp.maximum(m, jnp.max(s, axis=-1, keepdims=True))
            alpha = jnp.exp2(m - m_new)
            p = jnp.exp2(s - m_new)
            l = alpha * l + jnp.sum(p, axis=-1, keepdims=True)
            acc = alpha * acc + _dot(p.astype(BF16), v)
            return m_new, l, acc

        init = (jnp.full((2 * tq, 1), MASK_VALUE, F32), jnp.zeros((2 * tq, 1), F32),
                jnp.zeros((2 * tq, DIFF_V_DIM), F32))
        carry = lax.fori_loop(0, i, functools.partial(online_step, masked=False), init)
        _, l, acc = online_step(i, carry, masked=True)
    else:
        qs_ref, acc_ref = scratch

        def pv(s, v):
            v_ones = jnp.concatenate([v, jnp.ones_like(v)], axis=1)
            return _dot(jnp.exp2(s).astype(BF16), v_ones)

        h = tq // 2
        k0, v0 = kv(i * tq, h)
        k1, v1 = kv(i * tq + h, h)
        s0 = _dot_nt(qs, k0)
        s0 = jnp.concatenate([causal_mask(s0[:h], h), s0[h:tq],
                              causal_mask(s0[tq:tq + h], h), s0[tq + h:]], axis=0)
        acc_ref[...] = pv(s0, v0)
        qs_late = jnp.concatenate([qs[h:tq], qs[tq + h:]], axis=0)
        late = pv(causal_mask(_dot_nt(qs_late, k1), h), v1)
        acc_ref[h:tq, :] += late[:h]
        acc_ref[tq + h:, :] += late[h:]
        qs_ref[...] = qs

        @pl.loop(0, i)
        def _(j):
            k, v = kv(j * tq, tq)
            acc_ref[...] += pv(_dot_nt(qs_ref[...], k), v)

        acc, l = acc_ref[:, :DIFF_V_DIM], acc_ref[:, DIFF_V_DIM:]

    lam = (jnp.exp(jnp.sum(lq1_ref[...] * lk1_ref[...], axis=-1, keepdims=True))
           - jnp.exp(jnp.sum(lq2_ref[...] * lk2_ref[...], axis=-1, keepdims=True))
           + lam_init)
    o = acc[:tq] / l[:tq] - lam * (acc[tq:] / l[tq:])
    o_ref[0] = (_rms(o, sg_ref[...]) * (1.0 - lam_init)).astype(BF16)


def _diff_attn(proj3d, lq1, lk1, lq2, lk2, subln_g, lam_init, tq, online):
    b, s, _ = proj3d.shape
    vec = pl.BlockSpec((1, DIFF_HEAD_DIM), lambda bi, h, i: (0, 0))
    return pl.pallas_call(
        functools.partial(_diff_attn_kernel, tq=tq, lam_init=lam_init, online=online),
        grid=(b, DIFF_HEADS, s // tq),
        in_specs=[
            vec, vec, vec, vec,
            pl.BlockSpec((1, DIFF_V_DIM), lambda bi, h, i: (0, 0)),
            pl.BlockSpec((1, tq, DIFF_V_DIM), lambda bi, h, i: (bi, i, COL_Q * DIFF_HEADS + h)),
            pl.BlockSpec((1, s, DIFF_V_DIM), lambda bi, h, i: (bi, 0, COL_K * DIFF_HEADS + h)),
            pl.BlockSpec((1, s, DIFF_V_DIM), lambda bi, h, i: (bi, 0, COL_V * DIFF_HEADS + h)),
        ],
        out_specs=pl.BlockSpec((1, tq, DIFF_V_DIM), lambda bi, h, i: (bi, i, h)),
        out_shape=jax.ShapeDtypeStruct((b, s, DIFF_HEADS * DIFF_V_DIM), BF16),
        scratch_shapes=[] if online else [pltpu.VMEM((2 * tq, DIFF_V_DIM), BF16),
                                          pltpu.VMEM((2 * tq, 2 * DIFF_V_DIM), F32)],
        compiler_params=pltpu.CompilerParams(
            dimension_semantics=("parallel", "parallel", "parallel"),
            vmem_limit_bytes=VMEM_LIMIT),
        name="diff_attn_online" if online else "diff_attn",
    )(lq1, lk1, lq2, lk2, subln_g, proj3d, proj3d, proj3d)


def _merge_kernel(x_ref, on_ref, xc_ref, gb_ref, gc_ref, hxc_ref, hgc_ref, qm_ref,
                  la_ref, lc_ref, lm_ref, kt_ref, vm_ref, cw_ref, bg_ref,
                  wa_ref, wc_ref, wm_ref, wo_ref, o_ref, inner_scr, *, tiles_per_seq):
    i = pl.program_id(0)
    tm, d = x_ref.shape

    ya = _dot(on_ref[...], wa_ref[...])

    first = (i % tiles_per_seq) == 0
    halo = hgc_ref[...].astype(F32) * hxc_ref[...].astype(F32)
    inner_scr[0:HALO, :] = jnp.where(first, 0.0, halo)
    inner = gc_ref[...].astype(F32) * xc_ref[...].astype(F32)
    inner_scr[HALO:HALO + tm, :] = inner
    conv = (cw_ref[2:3, :] * inner
            + cw_ref[1:2, :] * inner_scr[HALO - 1:HALO - 1 + tm, :]
            + cw_ref[0:1, :] * inner_scr[HALO - 2:HALO - 2 + tm, :])
    yc = _dot((gb_ref[...].astype(F32) * conv).astype(BF16), wc_ref[...])

    heads = []
    for h in range(MEM_HEADS):
        sl = slice(h * MEM_HEAD_DIM, (h + 1) * MEM_HEAD_DIM)
        s = _dot(qm_ref[:, sl], kt_ref[0, sl, :])
        p = jnp.exp(s - jnp.max(s, axis=-1, keepdims=True))
        p = p / jnp.sum(p, axis=-1, keepdims=True)
        heads.append(_dot(p.astype(BF16), vm_ref[0, :, sl]).astype(BF16))
    ym = _dot(jnp.concatenate(heads, axis=-1), wm_ref[...])

    ga = jax.nn.sigmoid(la_ref[...].astype(F32) + bg_ref[:, 0:d])
    gc = jax.nn.sigmoid(lc_ref[...].astype(F32) + bg_ref[:, d:2 * d])
    gm = jax.nn.sigmoid(lm_ref[...].astype(F32) + bg_ref[:, 2 * d:3 * d])
    merged = ga * ya + gc * yc + gm * ym
    o_ref[...] = x_ref[...] + _dot(merged.astype(BF16), wo_ref[...])


def _merge(x2d, on2d, proj, kt, vm, conv_w, b_gate, wa, wc, wm, wo, tm, seq):
    m, d = x2d.shape
    tiles_per_seq = seq // tm
    halo_blocks = tm // HALO

    def col(c):
        return pl.BlockSpec((tm, d), lambda i: (i, c))

    def halo(c):
        return pl.BlockSpec((HALO, d), lambda i: (jnp.maximum(i * halo_blocks - 1, 0), c))

    def whole(shape):
        return pl.BlockSpec(shape, lambda i: (0,) * len(shape), pipeline_mode=pl.Buffered(1))

    mem_len = vm.shape[1]
    return pl.pallas_call(
        functools.partial(_merge_kernel, tiles_per_seq=tiles_per_seq),
        grid=(m // tm,),
        in_specs=[
            pl.BlockSpec((tm, d), lambda i: (i, 0)),
            pl.BlockSpec((tm, d), lambda i: (i, 0)),
            col(COL_XC), col(COL_GB), col(COL_GC), halo(COL_XC), halo(COL_GC), col(COL_QM),
            col(COL_GATE), col(COL_GATE + 1), col(COL_GATE + 2),
            pl.BlockSpec((1, d, mem_len), lambda i: (i // tiles_per_seq, 0, 0)),
            pl.BlockSpec((1, mem_len, d), lambda i: (i // tiles_per_seq, 0, 0)),
            whole((CONV_K, d)), whole((1, 3 * d)),
            whole((d, d)), whole((d, d)), whole((d, d)), whole((d, d)),
        ],
        out_specs=pl.BlockSpec((tm, d), lambda i: (i, 0)),
        out_shape=jax.ShapeDtypeStruct((m, d), F32),
        scratch_shapes=[pltpu.VMEM((HALO + tm, d), F32)],
        compiler_params=pltpu.CompilerParams(
            dimension_semantics=("parallel",), vmem_limit_bytes=VMEM_LIMIT),
        name="merge",
    )(x2d, on2d, proj, proj, proj, proj, proj, proj, proj, proj, proj,
      kt, vm, conv_w, b_gate, wa, wc, wm, wo)


def _mlp_kernel(x_ref, g_ref, w1_ref, w2_ref, o_ref, h_scr):
    f = pl.program_id(1)

    @pl.when(f == 0)
    def _():
        x = x_ref[...]
        h_scr[...] = _rms(x, g_ref[...]).astype(BF16)
        o_ref[...] = x

    u = jnp.square(jnp.maximum(_dot(h_scr[...], w1_ref[...]), 0.0))
    o_ref[...] += _dot(u.astype(BF16), w2_ref[...])


def _mlp(x2d, g, w1, w2, tm, tf):
    m, d = x2d.shape
    dff = w1.shape[1]
    return pl.pallas_call(
        _mlp_kernel,
        grid=(m // tm, dff // tf),
        in_specs=[
            pl.BlockSpec((tm, d), lambda i, f: (i, 0)),
            pl.BlockSpec((1, d), lambda i, f: (0, 0)),
            pl.BlockSpec((d, tf), lambda i, f: (0, f)),
            pl.BlockSpec((tf, d), lambda i, f: (f, 0)),
        ],
        out_specs=pl.BlockSpec((tm, d), lambda i, f: (i, 0)),
        out_shape=jax.ShapeDtypeStruct((m, d), F32),
        scratch_shapes=[pltpu.VMEM((tm, d), BF16)],
        compiler_params=pltpu.CompilerParams(
            dimension_semantics=("parallel", "arbitrary"), vmem_limit_bytes=VMEM_LIMIT),
        name="mlp",
    )(x2d, g, w1, w2)


def _group_mean_matrix(group):
    idx = np.arange(MXU_EDGE) // group
    return (idx[:, None] == idx[None, :]).astype(np.float32) / group


def _layer(x, mem, lam_init, norm_mix_g, norm_mem_g, w_in, b_gate, q_norm_g, k_norm_g,
           lam_q1, lam_k1, lam_q2, lam_k2, subln_g, w_attn_o, conv_w, w_conv_o, w_mem_kv,
           mq_norm_g, mk_norm_g, w_mem_o, w_o, norm_mlp_g, w_mlp_in, w_mlp_out):
    b, s, d = x.shape
    x2d = x.reshape(b * s, d)
    row = lambda v: v.reshape(1, -1)

    e_mats = jnp.asarray(np.stack([_group_mean_matrix(DIFF_HEAD_DIM),
                                   _group_mean_matrix(DIFF_HEAD_DIM),
                                   _group_mean_matrix(MEM_HEAD_DIM)]), BF16)
    gn = jnp.stack([
        jnp.tile(q_norm_g * (LOG2_E * DIFF_HEAD_DIM ** -0.5), d // DIFF_HEAD_DIM),
        jnp.tile(k_norm_g, d // DIFF_HEAD_DIM),
        jnp.tile(mq_norm_g * MEM_HEAD_DIM ** -0.5, d // MEM_HEAD_DIM),
    ]).reshape(3, 1, d)

    kt, vm = _mem_kv(mem, row(norm_mem_g), w_mem_kv.astype(BF16), row(mk_norm_g))
    proj = _in_proj(x2d, row(norm_mix_g), w_in.astype(BF16), e_mats, gn, tm=1024)

    score_bound = DIFF_HEAD_DIM ** 0.5 * jnp.max(jnp.abs(q_norm_g)) * jnp.max(jnp.abs(k_norm_g))
    attn = functools.partial(_diff_attn, proj.reshape(b, s, -1), row(lam_q1), row(lam_k1),
                             row(lam_q2), row(lam_k2), row(subln_g), lam_init)
    o_n = lax.cond(score_bound <= MAX_UNSHIFTED_LOGIT,
                   lambda: attn(tq=1024, online=False), lambda: attn(tq=512, online=True))
    x1 = _merge(x2d, o_n.reshape(b * s, d), proj, kt, vm, conv_w, row(b_gate),
                w_attn_o.astype(BF16), w_conv_o.astype(BF16), w_mem_o.astype(BF16),
                w_o.astype(BF16), tm=512, seq=s)
    x2 = _mlp(x1, row(norm_mlp_g), w_mlp_in.astype(BF16), w_mlp_out.astype(BF16), tm=1024, tf=1024)
    return x2.reshape(b, s, d)


def kernel(x, mem, norm_mix_g, norm_mem_g, w_in, b_gate, q_norm_g, k_norm_g, lam_q1, lam_k1,
           lam_q2, lam_k2, subln_g, w_attn_o, conv_w, w_conv_o, w_mem_kv, mq_norm_g, mk_norm_g,
           w_mem_o, w_o, norm_mlp_g, w_mlp_in, w_mlp_out):
    depth = w_in.shape[0]
    for layer in range(depth):
        lam_init = 0.8 - 0.6 * float(np.exp(-0.3 * layer))
        x = _layer(x, mem, lam_init, norm_mix_g[layer], norm_mem_g[layer], w_in[layer],
                   b_gate[layer], q_norm_g[layer], k_norm_g[layer], lam_q1[layer], lam_k1[layer],
                   lam_q2[layer], lam_k2[layer], subln_g[layer], w_attn_o[layer], conv_w[layer],
                   w_conv_o[layer], w_mem_kv[layer], mq_norm_g[layer], mk_norm_g[layer],
                   w_mem_o[layer], w_o[layer], norm_mlp_g[layer], w_mlp_in[layer],
                   w_mlp_out[layer])
    return x
```

```python
import functools

import jax
import jax.numpy as jnp
import numpy as np
from jax import lax
from jax.experimental import pallas as pl
from jax.experimental.pallas import tpu as pltpu

F32 = jnp.float32
BF16 = jnp.bfloat16

NORM_EPS = 1e-6
MASK_VALUE = -1e30
LOG2_E = 1.4426950408889634
MAX_UNSHIFTED_LOGIT = 60.0

DIFF_HEADS = 8
DIFF_HEAD_DIM = 64
DIFF_V_DIM = 2 * DIFF_HEAD_DIM
MEM_HEADS = 4
MEM_HEAD_DIM = 256
CONV_K = 3

MXU_EDGE = 256
HALO = 8

W_Q, W_K, W_V, W_XC, W_GB, W_GC, W_QM, W_GATE = 0, 1, 2, 3, 4, 5, 6, 7
COL_Q, COL_K, COL_V, COL_YC, COL_QM, COL_GATE = 0, 1, 2, 3, 4, 5
N_COL_BLOCKS = 8

VMEM_LIMIT = 56 * 1024 * 1024


def _rms(x, g):
    ms = jnp.mean(x * x, axis=-1, keepdims=True)
    return x * lax.rsqrt(ms + NORM_EPS) * g


def _dot(a, b):
    return jnp.dot(a, b, preferred_element_type=F32)


def _dot_nt(a, b):
    return lax.dot_general(a, b, (((1,), (1,)), ((), ())), preferred_element_type=F32)


def _mem_kv_kernel(mem_ref, g_ref, w_ref, kg_ref, kt_ref, v_ref):
    mem_n = _rms(mem_ref[0], g_ref[...]).astype(BF16)
    kv = _dot(mem_n, w_ref[...])
    width = kt_ref.shape[1]
    for h in range(MEM_HEADS):
        sl = slice(h * MEM_HEAD_DIM, (h + 1) * MEM_HEAD_DIM)
        kn = _rms(kv[:, sl], kg_ref[...])
        kt_ref[0, sl, :] = kn.T.astype(BF16)
    v_ref[0] = kv[:, width:].astype(BF16)


def _mem_kv(mem, g, w_kv, kg):
    b, m, d = mem.shape
    width = w_kv.shape[1] // 2
    return pl.pallas_call(
        _mem_kv_kernel,
        grid=(b,),
        in_specs=[
            pl.BlockSpec((1, m, d), lambda i: (i, 0, 0)),
            pl.BlockSpec((1, d), lambda i: (0, 0)),
            pl.BlockSpec((d, 2 * width), lambda i: (0, 0)),
            pl.BlockSpec((1, MEM_HEAD_DIM), lambda i: (0, 0)),
        ],
        out_specs=[
            pl.BlockSpec((1, width, m), lambda i: (i, 0, 0)),
            pl.BlockSpec((1, m, width), lambda i: (i, 0, 0)),
        ],
        out_shape=[
            jax.ShapeDtypeStruct((b, width, m), BF16),
            jax.ShapeDtypeStruct((b, m, width), BF16),
        ],
        compiler_params=pltpu.CompilerParams(
            dimension_semantics=("parallel",), vmem_limit_bytes=VMEM_LIMIT),
        name="mem_kv",
    )(mem, g, w_kv, kg)


def _in_proj_kernel(x_ref, g_ref, w_ref, e_ref, gn_ref, cw_ref, bg_ref, o_ref, inner_scr,
                    *, tiles_per_seq):
    i = pl.program_id(0)
    tm, d = x_ref.shape
    h = _rms(x_ref[...], g_ref[...]).astype(BF16)

    def proj(c):
        return _dot(h, w_ref[:, c * d:(c + 1) * d])

    def put(c, val):
        o_ref[:, c * d:(c + 1) * d] = val.astype(BF16)

    for slot, (src, dst) in enumerate(((W_Q, COL_Q), (W_K, COL_K), (W_QM, COL_QM))):
        acc = proj(src)
        sq = (acc * acc).astype(BF16)
        for c in range(d // MXU_EDGE):
            sl = slice(c * MXU_EDGE, (c + 1) * MXU_EDGE)
            ms = _dot(sq[:, sl], e_ref[slot])
            y = acc[:, sl] * lax.rsqrt(ms + NORM_EPS) * gn_ref[slot, :, sl]
            o_ref[:, dst * d + c * MXU_EDGE:dst * d + (c + 1) * MXU_EDGE] = y.astype(BF16)

    put(COL_V, proj(W_V))

    @pl.when(i % tiles_per_seq == 0)
    def _():
        inner_scr[0:HALO, :] = jnp.zeros((HALO, d), F32)

    inner = proj(W_GC) * proj(W_XC)
    inner_scr[HALO:HALO + tm, :] = inner
    conv = (cw_ref[2:3, :] * inner
            + cw_ref[1:2, :] * inner_scr[HALO - 1:HALO - 1 + tm, :]
            + cw_ref[0:1, :] * inner_scr[HALO - 2:HALO - 2 + tm, :])
    put(COL_YC, proj(W_GB) * conv)
    inner_scr[0:HALO, :] = inner_scr[tm:tm + HALO, :]

    for t in range(3):
        put(COL_GATE + t, jax.nn.sigmoid(proj(W_GATE + t) + bg_ref[:, t * d:(t + 1) * d]))


def _resident(shape):
    return pl.BlockSpec(shape, lambda i: (0,) * len(shape), pipeline_mode=pl.Buffered(1))


def _in_proj(x2d, g, w_in, e_mats, gn, conv_w, b_gate, tm, seq):
    m, d = x2d.shape
    return pl.pallas_call(
        functools.partial(_in_proj_kernel, tiles_per_seq=seq // tm),
        grid=(m // tm,),
        in_specs=[
            pl.BlockSpec((tm, d), lambda i: (i, 0)),
            _resident((1, d)), _resident(w_in.shape), _resident(e_mats.shape),
            _resident(gn.shape), _resident(conv_w.shape), _resident(b_gate.shape),
        ],
        out_specs=pl.BlockSpec((tm, N_COL_BLOCKS * d), lambda i: (i, 0)),
        out_shape=jax.ShapeDtypeStruct((m, N_COL_BLOCKS * d), BF16),
        scratch_shapes=[pltpu.VMEM((HALO + tm, d), F32)],
        compiler_params=pltpu.CompilerParams(
            dimension_semantics=("arbitrary",), vmem_limit_bytes=VMEM_LIMIT),
        name="in_proj",
    )(x2d, g, w_in, e_mats, gn, conv_w, b_gate)


def _diff_attn_kernel(lq1_ref, lk1_ref, lq2_ref, lk2_ref, sg_ref, q_ref, k_ref, v_ref, o_ref,
                      *scratch, tq, lam_init, online):
    i = pl.program_id(2)
    q = q_ref[0]
    lane = lax.broadcasted_iota(jnp.int32, q.shape, 1)
    zero = jnp.zeros_like(q)
    qs = jnp.concatenate([jnp.where(lane < DIFF_HEAD_DIM, q, zero),
                          jnp.where(lane >= DIFF_HEAD_DIM, q, zero)], axis=0)

    def kv(start, size):
        start = pl.multiple_of(start, size)
        return k_ref[0, pl.ds(start, size), :], v_ref[0, pl.ds(start, size), :]

    def causal_mask(s, n):
        row = lax.broadcasted_iota(jnp.int32, (n, n), 0)
        col = lax.broadcasted_iota(jnp.int32, (n, n), 1)
        keep = jnp.concatenate([col <= row] * (s.shape[0] // n), axis=0)
        return jnp.where(keep, s, MASK_VALUE)

    if online:
        def online_step(j, carry, masked):
            m, l, acc = carry
            k, v = kv(j * tq, tq)
            s = _dot_nt(qs, k)
            if masked:
                s = causal_mask(s, tq)
            m_new = jnp.maximum(m, jnp.max(s, axis=-1, keepdims=True))
            alpha = jnp.exp2(m - m_new)
            p = jnp.exp2(s - m_new)
            l = alpha * l + jnp.sum(p, axis=-1, keepdims=True)
            acc = alpha * acc + _dot(p.astype(BF16), v)
            return m_new, l, acc

        init = (jnp.full((2 * tq, 1), MASK_VALUE, F32), jnp.zeros((2 * tq, 1), F32),
                jnp.zeros((2 * tq, DIFF_V_DIM), F32))
        carry = lax.fori_loop(0, i, functools.partial(online_step, masked=False), init)
        _, l, acc = online_step(i, carry, masked=True)
    else:
        qs_ref, acc_ref = scratch

        def pv(s, v):
            v_ones = jnp.concatenate([v, jnp.ones_like(v)], axis=1)
            return _dot(jnp.exp2(s).astype(BF16), v_ones)

        h = tq // 2
        k0, v0 = kv(i * tq, h)
        k1, v1 = kv(i * tq + h, h)
        s0 = _dot_nt(qs, k0)
        s0 = jnp.concatenate([causal_mask(s0[:h], h), s0[h:tq],
                              causal_mask(s0[tq:tq + h], h), s0[tq + h:]], axis=0)
        acc_ref[...] = pv(s0, v0)
        qs_late = jnp.concatenate([qs[h:tq], qs[tq + h:]], axis=0)
        late = pv(causal_mask(_dot_nt(qs_late, k1), h), v1)
        acc_ref[h:tq, :] += late[:h]
        acc_ref[tq + h:, :] += late[h:]
        qs_ref[...] = qs

        @pl.loop(0, i)
        def _(j):
            k, v = kv(j * tq, tq)
            acc_ref[...] += pv(_dot_nt(qs_ref[...], k), v)

        acc, l = acc_ref[:, :DIFF_V_DIM], acc_ref[:, DIFF_V_DIM:]

    lam = (jnp.exp(jnp.sum(lq1_ref[...] * lk1_ref[...], axis=-1, keepdims=True))
           - jnp.exp(jnp.sum(lq2_ref[...] * lk2_ref[...], axis=-1, keepdims=True))
           + lam_init)
    o = acc[:tq] / l[:tq] - lam * (acc[tq:] / l[tq:])
    o_ref[0] = (_rms(o, sg_ref[...]) * (1.0 - lam_init)).astype(BF16)


def _diff_attn(proj3d, lq1, lk1, lq2, lk2, subln_g, lam_init, tq, online):
    b, s, _ = proj3d.shape
    vec = pl.BlockSpec((1, DIFF_HEAD_DIM), lambda bi, h, i: (0, 0))
    return pl.pallas_call(
        functools.partial(_diff_attn_kernel, tq=tq, lam_init=lam_init, online=online),
        grid=(b, DIFF_HEADS, s // tq),
        in_specs=[
            vec, vec, vec, vec,
            pl.BlockSpec((1, DIFF_V_DIM), lambda bi, h, i: (0, 0)),
            pl.BlockSpec((1, tq, DIFF_V_DIM), lambda bi, h, i: (bi, i, COL_Q * DIFF_HEADS + h)),
            pl.BlockSpec((1, s, DIFF_V_DIM), lambda bi, h, i: (bi, 0, COL_K * DIFF_HEADS + h)),
            pl.BlockSpec((1, s, DIFF_V_DIM), lambda bi, h, i: (bi, 0, COL_V * DIFF_HEADS + h)),
        ],
        out_specs=pl.BlockSpec((1, tq, DIFF_V_DIM), lambda bi, h, i: (bi, i, h)),
        out_shape=jax.ShapeDtypeStruct((b, s, DIFF_HEADS * DIFF_V_DIM), BF16),
        scratch_shapes=[] if online else [pltpu.VMEM((2 * tq, DIFF_V_DIM), BF16),
                                          pltpu.VMEM((2 * tq, 2 * DIFF_V_DIM), F32)],
        compiler_params=pltpu.CompilerParams(
            dimension_semantics=("parallel", "parallel", "parallel"),
            vmem_limit_bytes=VMEM_LIMIT),
        name="diff_attn_online" if online else "diff_attn",
    )(lq1, lk1, lq2, lk2, subln_g, proj3d, proj3d, proj3d)


def _merge_kernel(x_ref, on_ref, yc_ref, qm_ref, ga_ref, gc_ref, gm_ref, kt_ref, vm_ref,
                  wa_ref, wc_ref, wm_ref, wo_ref, o_ref):
    ya = _dot(on_ref[...], wa_ref[...])
    yc = _dot(yc_ref[...], wc_ref[...])

    heads = []
    for h in range(MEM_HEADS):
        sl = slice(h * MEM_HEAD_DIM, (h + 1) * MEM_HEAD_DIM)
        s = _dot(qm_ref[:, sl], kt_ref[0, sl, :])
        p = jnp.exp(s - jnp.max(s, axis=-1, keepdims=True))
        p = p / jnp.sum(p, axis=-1, keepdims=True)
        heads.append(_dot(p.astype(BF16), vm_ref[0, :, sl]).astype(BF16))
    ym = _dot(jnp.concatenate(heads, axis=-1), wm_ref[...])

    merged = (ga_ref[...].astype(F32) * ya + gc_ref[...].astype(F32) * yc
              + gm_ref[...].astype(F32) * ym)
    o_ref[...] = x_ref[...] + _dot(merged.astype(BF16), wo_ref[...])


def _merge(x2d, on2d, proj, kt, vm, wa, wc, wm, wo, tm, seq):
    m, d = x2d.shape
    tiles_per_seq = seq // tm

    def col(c):
        return pl.BlockSpec((tm, d), lambda i: (i, c))

    mem_len = vm.shape[1]
    return pl.pallas_call(
        _merge_kernel,
        grid=(m // tm,),
        in_specs=[
            col(0), col(0),
            col(COL_YC), col(COL_QM), col(COL_GATE), col(COL_GATE + 1), col(COL_GATE + 2),
            pl.BlockSpec((1, d, mem_len), lambda i: (i // tiles_per_seq, 0, 0)),
            pl.BlockSpec((1, mem_len, d), lambda i: (i // tiles_per_seq, 0, 0)),
            _resident((d, d)), _resident((d, d)), _resident((d, d)), _resident((d, d)),
        ],
        out_specs=col(0),
        out_shape=jax.ShapeDtypeStruct((m, d), F32),
        compiler_params=pltpu.CompilerParams(
            dimension_semantics=("parallel",), vmem_limit_bytes=VMEM_LIMIT),
        name="merge",
    )(x2d, on2d, proj, proj, proj, proj, proj, kt, vm, wa, wc, wm, wo)


def _mlp_kernel(x_ref, g_ref, w1_ref, w2_ref, o_ref, *, tf):
    x = x_ref[...]
    h = _rms(x, g_ref[...]).astype(BF16)
    acc = x
    for f in range(w1_ref.shape[1] // tf):
        u = jnp.square(jnp.maximum(_dot(h, w1_ref[:, f * tf:(f + 1) * tf]), 0.0))
        acc = acc + _dot(u.astype(BF16), w2_ref[f * tf:(f + 1) * tf, :])
    o_ref[...] = acc


def _mlp(x2d, g, w1, w2, tm, tf):
    m, d = x2d.shape
    return pl.pallas_call(
        functools.partial(_mlp_kernel, tf=tf),
        grid=(m // tm,),
        in_specs=[
            pl.BlockSpec((tm, d), lambda i: (i, 0)),
            _resident((1, d)), _resident(w1.shape), _resident(w2.shape),
        ],
        out_specs=pl.BlockSpec((tm, d), lambda i: (i, 0)),
        out_shape=jax.ShapeDtypeStruct((m, d), F32),
        compiler_params=pltpu.CompilerParams(
            dimension_semantics=("parallel",), vmem_limit_bytes=VMEM_LIMIT),
        name="mlp",
    )(x2d, g, w1, w2)


def _group_mean_matrix(group):
    idx = np.arange(MXU_EDGE) // group
    return (idx[:, None] == idx[None, :]).astype(np.float32) / group


def _layer(x, mem, lam_init, norm_mix_g, norm_mem_g, w_in, b_gate, q_norm_g, k_norm_g,
           lam_q1, lam_k1, lam_q2, lam_k2, subln_g, w_attn_o, conv_w, w_conv_o, w_mem_kv,
           mq_norm_g, mk_norm_g, w_mem_o, w_o, norm_mlp_g, w_mlp_in, w_mlp_out):
    b, s, d = x.shape
    x2d = x.reshape(b * s, d)
    row = lambda v: v.reshape(1, -1)

    e_mats = jnp.asarray(np.stack([_group_mean_matrix(DIFF_HEAD_DIM),
                                   _group_mean_matrix(DIFF_HEAD_DIM),
                                   _group_mean_matrix(MEM_HEAD_DIM)]), BF16)
    gn = jnp.stack([
        jnp.tile(q_norm_g * (LOG2_E * DIFF_HEAD_DIM ** -0.5), d // DIFF_HEAD_DIM),
        jnp.tile(k_norm_g, d // DIFF_HEAD_DIM),
        jnp.tile(mq_norm_g * MEM_HEAD_DIM ** -0.5, d // MEM_HEAD_DIM),
    ]).reshape(3, 1, d)

    kt, vm = _mem_kv(mem, row(norm_mem_g), w_mem_kv.astype(BF16), row(mk_norm_g))
    proj = _in_proj(x2d, row(norm_mix_g), w_in.astype(BF16), e_mats, gn, conv_w, row(b_gate),
                    tm=512, seq=s)

    score_bound = DIFF_HEAD_DIM ** 0.5 * jnp.max(jnp.abs(q_norm_g)) * jnp.max(jnp.abs(k_norm_g))
    attn = functools.partial(_diff_attn, proj.reshape(b, s, -1), row(lam_q1), row(lam_k1),
                             row(lam_q2), row(lam_k2), row(subln_g), lam_init)
    o_n = lax.cond(score_bound <= MAX_UNSHIFTED_LOGIT,
                   lambda: attn(tq=1024, online=False), lambda: attn(tq=512, online=True))
    x1 = _merge(x2d, o_n.reshape(b * s, d), proj, kt, vm,
                w_attn_o.astype(BF16), w_conv_o.astype(BF16), w_mem_o.astype(BF16),
                w_o.astype(BF16), tm=512, seq=s)
    x2 = _mlp(x1, row(norm_mlp_g), w_mlp_in.astype(BF16), w_mlp_out.astype(BF16), tm=1024, tf=1024)
    return x2.reshape(b, s, d)


def kernel(x, mem, norm_mix_g, norm_mem_g, w_in, b_gate, q_norm_g, k_norm_g, lam_q1, lam_k1,
           lam_q2, lam_k2, subln_g, w_attn_o, conv_w, w_conv_o, w_mem_kv, mq_norm_g, mk_norm_g,
           w_mem_o, w_o, norm_mlp_g, w_mlp_in, w_mlp_out):
    depth = w_in.shape[0]
    for layer in range(depth):
        lam_init = 0.8 - 0.6 * float(np.exp(-0.3 * layer))
        x = _layer(x, mem, lam_init, norm_mix_g[layer], norm_mem_g[layer], w_in[layer],
                   b_gate[layer], q_norm_g[layer], k_norm_g[layer], lam_q1[layer], lam_k1[layer],
                   lam_q2[layer], lam_k2[layer], subln_g[layer], w_attn_o[layer], conv_w[layer],
                   w_conv_o[layer], w_mem_kv[layer], mq_norm_g[layer], mk_norm_g[layer],
                   w_mem_o[layer], w_o[layer], norm_mlp_g[layer], w_mlp_in[layer],
                   w_mlp_out[layer])
    return x
```

```python
import functools

import jax
import jax.numpy as jnp
import numpy as np
from jax import lax
from jax.experimental import pallas as pl
from jax.experimental.pallas import tpu as pltpu

F32 = jnp.float32
BF16 = jnp.bfloat16

NORM_EPS = 1e-6
MASK_VALUE = -1e30
LOG2_E = 1.4426950408889634
MAX_UNSHIFTED_LOGIT = 60.0

DIFF_HEADS = 8
DIFF_HEAD_DIM = 64
DIFF_V_DIM = 2 * DIFF_HEAD_DIM
MEM_HEADS = 4
MEM_HEAD_DIM = 256
CONV_K = 3

MXU_EDGE = 256
HALO = 8

W_Q, W_K, W_V, W_XC, W_GB, W_GC, W_QM, W_GATE = 0, 1, 2, 3, 4, 5, 6, 7
COL_Q, COL_K, COL_V, COL_YC, COL_QM, COL_GATE = 0, 1, 2, 3, 4, 5
N_COL_BLOCKS = 8

VMEM_LIMIT = 56 * 1024 * 1024


def _rms(x, g):
    ms = jnp.mean(x * x, axis=-1, keepdims=True)
    return x * lax.rsqrt(ms + NORM_EPS) * g


def _dot(a, b):
    return jnp.dot(a, b, preferred_element_type=F32)


def _dot_nt(a, b):
    return lax.dot_general(a, b, (((1,), (1,)), ((), ())), preferred_element_type=F32)


def _mem_kv_kernel(mem_ref, g_ref, w_ref, kg_ref, kt_ref, v_ref):
    mem_n = _rms(mem_ref[0], g_ref[...]).astype(BF16)
    kv = _dot(mem_n, w_ref[...])
    width = kt_ref.shape[1]
    for h in range(MEM_HEADS):
        sl = slice(h * MEM_HEAD_DIM, (h + 1) * MEM_HEAD_DIM)
        kn = _rms(kv[:, sl], kg_ref[...])
        kt_ref[0, sl, :] = kn.T.astype(BF16)
    v_ref[0] = kv[:, width:].astype(BF16)


def _mem_kv(mem, g, w_kv, kg):
    b, m, d = mem.shape
    width = w_kv.shape[1] // 2
    return pl.pallas_call(
        _mem_kv_kernel,
        grid=(b,),
        in_specs=[
            pl.BlockSpec((1, m, d), lambda i: (i, 0, 0)),
            pl.BlockSpec((1, d), lambda i: (0, 0)),
            pl.BlockSpec((d, 2 * width), lambda i: (0, 0)),
            pl.BlockSpec((1, MEM_HEAD_DIM), lambda i: (0, 0)),
        ],
        out_specs=[
            pl.BlockSpec((1, width, m), lambda i: (i, 0, 0)),
            pl.BlockSpec((1, m, width), lambda i: (i, 0, 0)),
        ],
        out_shape=[
            jax.ShapeDtypeStruct((b, width, m), BF16),
            jax.ShapeDtypeStruct((b, m, width), BF16),
        ],
        compiler_params=pltpu.CompilerParams(
            dimension_semantics=("parallel",), vmem_limit_bytes=VMEM_LIMIT),
        name="mem_kv",
    )(mem, g, w_kv, kg)


def _in_proj_kernel(x_ref, g_ref, w_ref, e_ref, gn_ref, cw_ref, bg_ref, o_ref, inner_scr,
                    *, tiles_per_seq):
    i = pl.program_id(0)
    tm, d = x_ref.shape

    @pl.when(i % tiles_per_seq == 0)
    def _():
        inner_scr[0:HALO, :] = jnp.zeros((HALO, d), F32)

    h = _rms(x_ref[...], g_ref[...]).astype(BF16)

    def proj(c):
        return _dot(h, w_ref[:, c * d:(c + 1) * d])

    def put(c, val):
        o_ref[:, c * d:(c + 1) * d] = val.astype(BF16)

    for t in range(3):
        put(COL_GATE + t, jax.nn.sigmoid(proj(W_GATE + t) + bg_ref[:, t * d:(t + 1) * d]))

    inner = proj(W_GC) * proj(W_XC)
    inner_scr[HALO:HALO + tm, :] = inner
    conv = (cw_ref[2:3, :] * inner
            + cw_ref[1:2, :] * inner_scr[HALO - 1:HALO - 1 + tm, :]
            + cw_ref[0:1, :] * inner_scr[HALO - 2:HALO - 2 + tm, :])
    put(COL_YC, proj(W_GB) * conv)
    inner_scr[0:HALO, :] = inner_scr[tm:tm + HALO, :]

    for slot, (src, dst) in enumerate(((W_Q, COL_Q), (W_K, COL_K), (W_QM, COL_QM))):
        acc = proj(src)
        sq = (acc * acc).astype(BF16)
        for c in range(d // MXU_EDGE):
            sl = slice(c * MXU_EDGE, (c + 1) * MXU_EDGE)
            ms = _dot(sq[:, sl], e_ref[slot])
            y = acc[:, sl] * lax.rsqrt(ms + NORM_EPS) * gn_ref[slot, :, sl]
            o_ref[:, dst * d + c * MXU_EDGE:dst * d + (c + 1) * MXU_EDGE] = y.astype(BF16)

    put(COL_V, proj(W_V))


def _resident(shape):
    return pl.BlockSpec(shape, lambda i: (0,) * len(shape), pipeline_mode=pl.Buffered(1))


def _in_proj(x2d, g, w_in, e_mats, gn, conv_w, b_gate, tm, seq):
    m, d = x2d.shape
    return pl.pallas_call(
        functools.partial(_in_proj_kernel, tiles_per_seq=seq // tm),
        grid=(m // tm,),
        in_specs=[
            pl.BlockSpec((tm, d), lambda i: (i, 0)),
            _resident((1, d)), _resident(w_in.shape), _resident(e_mats.shape),
            _resident(gn.shape), _resident(conv_w.shape), _resident(b_gate.shape),
        ],
        out_specs=pl.BlockSpec((tm, N_COL_BLOCKS * d), lambda i: (i, 0)),
        out_shape=jax.ShapeDtypeStruct((m, N_COL_BLOCKS * d), BF16),
        scratch_shapes=[pltpu.VMEM((HALO + tm, d), F32)],
        compiler_params=pltpu.CompilerParams(
            dimension_semantics=("arbitrary",), vmem_limit_bytes=VMEM_LIMIT),
        name="in_proj",
    )(x2d, g, w_in, e_mats, gn, conv_w, b_gate)


def _stack_maps(q):
    lane = lax.broadcasted_iota(jnp.int32, q.shape, 1)
    zero = jnp.zeros_like(q)
    return jnp.concatenate([jnp.where(lane < DIFF_HEAD_DIM, q, zero),
                            jnp.where(lane >= DIFF_HEAD_DIM, q, zero)], axis=0)


def _causal_mask(s, n):
    row = lax.broadcasted_iota(jnp.int32, (n, n), 0)
    col = lax.broadcasted_iota(jnp.int32, (n, n), 1)
    keep = jnp.concatenate([col <= row] * (s.shape[0] // n), axis=0)
    return jnp.where(keep, s, MASK_VALUE)


def _diff_lambda(lq1_ref, lk1_ref, lq2_ref, lk2_ref, lam_init):
    return (jnp.exp(jnp.sum(lq1_ref[...] * lk1_ref[...], axis=-1, keepdims=True))
            - jnp.exp(jnp.sum(lq2_ref[...] * lk2_ref[...], axis=-1, keepdims=True))
            + lam_init)


def _diff_finalize(acc, l, lam, sg, lam_init):
    n = acc.shape[0] // 2
    o = acc[:n] / l[:n] - lam * (acc[n:] / l[n:])
    return (_rms(o, sg) * (1.0 - lam_init)).astype(BF16)


def _diff_attn_online_kernel(lq1_ref, lk1_ref, lq2_ref, lk2_ref, sg_ref, q_ref, k_ref, v_ref,
                             o_ref, *, tq, lam_init):
    i = pl.program_id(2)
    qs = _stack_maps(q_ref[0])

    def step(j, carry, masked):
        m, l, acc = carry
        start = pl.multiple_of(j * tq, tq)
        s = _dot_nt(qs, k_ref[0, pl.ds(start, tq), :])
        if masked:
            s = _causal_mask(s, tq)
        m_new = jnp.maximum(m, jnp.max(s, axis=-1, keepdims=True))
        alpha = jnp.exp2(m - m_new)
        p = jnp.exp2(s - m_new)
        l = alpha * l + jnp.sum(p, axis=-1, keepdims=True)
        acc = alpha * acc + _dot(p.astype(BF16), v_ref[0, pl.ds(start, tq), :])
        return m_new, l, acc

    init = (jnp.full((2 * tq, 1), MASK_VALUE, F32), jnp.zeros((2 * tq, 1), F32),
            jnp.zeros((2 * tq, DIFF_V_DIM), F32))
    carry = lax.fori_loop(0, i, functools.partial(step, masked=False), init)
    _, l, acc = step(i, carry, masked=True)
    lam = _diff_lambda(lq1_ref, lk1_ref, lq2_ref, lk2_ref, lam_init)
    o_ref[0] = _diff_finalize(acc, l, lam, sg_ref[...], lam_init)


def _diff_attn_plain_kernel(lq1_ref, lk1_ref, lq2_ref, lk2_ref, sg_ref, q_ref, k_ref, v_ref,
                            o_ref, *, tq, lam_init):
    seq = q_ref.shape[1]
    h = tq // 2
    lam = _diff_lambda(lq1_ref, lk1_ref, lq2_ref, lk2_ref, lam_init)

    def pv(s, start, size):
        v = v_ref[0, start:start + size, :]
        v_ones = jnp.concatenate([v, jnp.ones_like(v)], axis=1)
        return _dot(jnp.exp2(s).astype(BF16), v_ones)

    for i in range(seq // tq):
        r0 = i * tq
        qs = _stack_maps(q_ref[0, r0:r0 + tq, :])
        s0 = _dot_nt(qs, k_ref[0, r0:r0 + h, :])
        s0 = jnp.concatenate([_causal_mask(s0[:h], h), s0[h:tq],
                              _causal_mask(s0[tq:tq + h], h), s0[tq + h:]], axis=0)
        acc = pv(s0, r0, h)
        qs_late = jnp.concatenate([qs[h:tq], qs[tq + h:]], axis=0)
        s1 = _causal_mask(_dot_nt(qs_late, k_ref[0, r0 + h:r0 + tq, :]), h)
        late = pv(s1, r0 + h, h)
        acc = jnp.concatenate([acc[:h], acc[h:tq] + late[:h],
                               acc[tq:tq + h], acc[tq + h:] + late[h:]], axis=0)
        for j in range(i):
            acc = acc + pv(_dot_nt(qs, k_ref[0, j * tq:(j + 1) * tq, :]), j * tq, tq)
        o_ref[0, r0:r0 + tq, :] = _diff_finalize(acc[:, :DIFF_V_DIM], acc[:, DIFF_V_DIM:], lam,
                                                 sg_ref[...], lam_init)


def _diff_attn(proj3d, lq1, lk1, lq2, lk2, subln_g, lam_init, tq, online):
    b, s, _ = proj3d.shape
    rows = tq if online else s
    grid = (b, DIFF_HEADS, s // rows)
    vec = pl.BlockSpec((1, DIFF_HEAD_DIM), lambda bi, h, i: (0, 0))
    body = _diff_attn_online_kernel if online else _diff_attn_plain_kernel
    return pl.pallas_call(
        functools.partial(body, tq=tq, lam_init=lam_init),
        grid=grid,
        in_specs=[
            vec, vec, vec, vec,
            pl.BlockSpec((1, DIFF_V_DIM), lambda bi, h, i: (0, 0)),
            pl.BlockSpec((1, rows, DIFF_V_DIM), lambda bi, h, i: (bi, i, COL_Q * DIFF_HEADS + h)),
            pl.BlockSpec((1, s, DIFF_V_DIM), lambda bi, h, i: (bi, 0, COL_K * DIFF_HEADS + h)),
            pl.BlockSpec((1, s, DIFF_V_DIM), lambda bi, h, i: (bi, 0, COL_V * DIFF_HEADS + h)),
        ],
        out_specs=pl.BlockSpec((1, rows, DIFF_V_DIM), lambda bi, h, i: (bi, i, h)),
        out_shape=jax.ShapeDtypeStruct((b, s, DIFF_HEADS * DIFF_V_DIM), BF16),
        compiler_params=pltpu.CompilerParams(
            dimension_semantics=("parallel", "parallel", "parallel"),
            vmem_limit_bytes=VMEM_LIMIT),
        name="diff_attn_online" if online else "diff_attn",
    )(lq1, lk1, lq2, lk2, subln_g, proj3d, proj3d, proj3d)


def _merge_kernel(x_ref, on_ref, yc_ref, qm_ref, ga_ref, gc_ref, gm_ref, kt_ref, vm_ref,
                  wa_ref, wc_ref, wm_ref, wo_ref, o_ref):
    ya = _dot(on_ref[...], wa_ref[...])
    yc = _dot(yc_ref[...], wc_ref[...])

    heads = []
    for h in range(MEM_HEADS):
        sl = slice(h * MEM_HEAD_DIM, (h + 1) * MEM_HEAD_DIM)
        s = _dot(qm_ref[:, sl], kt_ref[0, sl, :])
        p = jnp.exp(s - jnp.max(s, axis=-1, keepdims=True))
        p = p / jnp.sum(p, axis=-1, keepdims=True)
        heads.append(_dot(p.astype(BF16), vm_ref[0, :, sl]).astype(BF16))
    ym = _dot(jnp.concatenate(heads, axis=-1), wm_ref[...])

    merged = (ga_ref[...].astype(F32) * ya + gc_ref[...].astype(F32) * yc
              + gm_ref[...].astype(F32) * ym)
    o_ref[...] = x_ref[...] + _dot(merged.astype(BF16), wo_ref[...])


def _merge(x2d, on2d, proj, kt, vm, wa, wc, wm, wo, tm, seq):
    m, d = x2d.shape
    tiles_per_seq = seq // tm

    def col(c):
        return pl.BlockSpec((tm, d), lambda i: (i, c))

    mem_len = vm.shape[1]
    return pl.pallas_call(
        _merge_kernel,
        grid=(m // tm,),
        in_specs=[
            col(0), col(0),
            col(COL_YC), col(COL_QM), col(COL_GATE), col(COL_GATE + 1), col(COL_GATE + 2),
            pl.BlockSpec((1, d, mem_len), lambda i: (i // tiles_per_seq, 0, 0)),
            pl.BlockSpec((1, mem_len, d), lambda i: (i // tiles_per_seq, 0, 0)),
            _resident((d, d)), _resident((d, d)), _resident((d, d)), _resident((d, d)),
        ],
        out_specs=col(0),
        out_shape=jax.ShapeDtypeStruct((m, d), F32),
        compiler_params=pltpu.CompilerParams(
            dimension_semantics=("parallel",), vmem_limit_bytes=VMEM_LIMIT),
        name="merge",
    )(x2d, on2d, proj, proj, proj, proj, proj, kt, vm, wa, wc, wm, wo)


def _mlp_kernel(x_ref, g_ref, w1_ref, w2_ref, o_ref, *, tf):
    x = x_ref[...]
    h = _rms(x, g_ref[...]).astype(BF16)
    acc = x
    for f in range(w1_ref.shape[1] // tf):
        u = jnp.square(jnp.maximum(_dot(h, w1_ref[:, f * tf:(f + 1) * tf]), 0.0))
        acc = acc + _dot(u.astype(BF16), w2_ref[f * tf:(f + 1) * tf, :])
    o_ref[...] = acc


def _mlp(x2d, g, w1, w2, tm, tf):
    m, d = x2d.shape
    return pl.pallas_call(
        functools.partial(_mlp_kernel, tf=tf),
        grid=(m // tm,),
        in_specs=[
            pl.BlockSpec((tm, d), lambda i: (i, 0)),
            _resident((1, d)), _resident(w1.shape), _resident(w2.shape),
        ],
        out_specs=pl.BlockSpec((tm, d), lambda i: (i, 0)),
        out_shape=jax.ShapeDtypeStruct((m, d), F32),
        compiler_params=pltpu.CompilerParams(
            dimension_semantics=("parallel",), vmem_limit_bytes=VMEM_LIMIT),
        name="mlp",
    )(x2d, g, w1, w2)


def _group_mean_matrix(group):
    idx = np.arange(MXU_EDGE) // group
    return (idx[:, None] == idx[None, :]).astype(np.float32) / group


def _layer(x, mem, lam_init, norm_mix_g, norm_mem_g, w_in, b_gate, q_norm_g, k_norm_g,
           lam_q1, lam_k1, lam_q2, lam_k2, subln_g, w_attn_o, conv_w, w_conv_o, w_mem_kv,
           mq_norm_g, mk_norm_g, w_mem_o, w_o, norm_mlp_g, w_mlp_in, w_mlp_out):
    b, s, d = x.shape
    x2d = x.reshape(b * s, d)
    row = lambda v: v.reshape(1, -1)

    e_mats = jnp.asarray(np.stack([_group_mean_matrix(DIFF_HEAD_DIM),
                                   _group_mean_matrix(DIFF_HEAD_DIM),
                                   _group_mean_matrix(MEM_HEAD_DIM)]), BF16)
    gn = jnp.stack([
        jnp.tile(q_norm_g * (LOG2_E * DIFF_HEAD_DIM ** -0.5), d // DIFF_HEAD_DIM),
        jnp.tile(k_norm_g, d // DIFF_HEAD_DIM),
        jnp.tile(mq_norm_g * MEM_HEAD_DIM ** -0.5, d // MEM_HEAD_DIM),
    ]).reshape(3, 1, d)

    kt, vm = _mem_kv(mem, row(norm_mem_g), w_mem_kv.astype(BF16), row(mk_norm_g))
    proj = _in_proj(x2d, row(norm_mix_g), w_in.astype(BF16), e_mats, gn, conv_w, row(b_gate),
                    tm=512, seq=s)

    score_bound = DIFF_HEAD_DIM ** 0.5 * jnp.max(jnp.abs(q_norm_g)) * jnp.max(jnp.abs(k_norm_g))
    attn = functools.partial(_diff_attn, proj.reshape(b, s, -1), row(lam_q1), row(lam_k1),
                             row(lam_q2), row(lam_k2), row(subln_g), lam_init)
    o_n = lax.cond(score_bound <= MAX_UNSHIFTED_LOGIT,
                   lambda: attn(tq=1024, online=False), lambda: attn(tq=512, online=True))
    x1 = _merge(x2d, o_n.reshape(b * s, d), proj, kt, vm,
                w_attn_o.astype(BF16), w_conv_o.astype(BF16), w_mem_o.astype(BF16),
                w_o.astype(BF16), tm=512, seq=s)
    x2 = _mlp(x1, row(norm_mlp_g), w_mlp_in.astype(BF16), w_mlp_out.astype(BF16), tm=1024, tf=1024)
    return x2.reshape(b, s, d)


def kernel(x, mem, norm_mix_g, norm_mem_g, w_in, b_gate, q_norm_g, k_norm_g, lam_q1, lam_k1,
           lam_q2, lam_k2, subln_g, w_attn_o, conv_w, w_conv_o, w_mem_kv, mq_norm_g, mk_norm_g,
           w_mem_o, w_o, norm_mlp_g, w_mlp_in, w_mlp_out):
    depth = w_in.shape[0]
    for layer in range(depth):
        lam_init = 0.8 - 0.6 * float(np.exp(-0.3 * layer))
        x = _layer(x, mem, lam_init, norm_mix_g[layer], norm_mem_g[layer], w_in[layer],
                   b_gate[layer], q_norm_g[layer], k_norm_g[layer], lam_q1[layer], lam_k1[layer],
                   lam_q2[layer], lam_k2[layer], subln_g[layer], w_attn_o[layer], conv_w[layer],
                   w_conv_o[layer], w_mem_kv[layer], mq_norm_g[layer], mk_norm_g[layer],
                   w_mem_o[layer], w_o[layer], norm_mlp_g[layer], w_mlp_in[layer],
                   w_mlp_out[layer])
    return x
```

```python
import functools

import jax
import jax.numpy as jnp
import numpy as np
from jax import lax
from jax.experimental import pallas as pl
from jax.experimental.pallas import tpu as pltpu

F32 = jnp.float32
BF16 = jnp.bfloat16

NORM_EPS = 1e-6
MASK_VALUE = -1e30
LOG2_E = 1.4426950408889634
MAX_UNSHIFTED_LOGIT = 60.0

DIFF_HEADS = 8
DIFF_HEAD_DIM = 64
DIFF_V_DIM = 2 * DIFF_HEAD_DIM
MEM_HEADS = 4
MEM_HEAD_DIM = 256
CONV_K = 3

MXU_EDGE = 256
HALO = 8

W_Q, W_K, W_V, W_XC, W_GB, W_GC, W_QM, W_GATE = 0, 1, 2, 3, 4, 5, 6, 7
COL_Q, COL_K, COL_V, COL_YC, COL_QM, COL_GATE = 0, 1, 2, 3, 4, 5
N_COL_BLOCKS = 8

VMEM_LIMIT = 56 * 1024 * 1024


def _rms(x, g):
    ms = jnp.mean(x * x, axis=-1, keepdims=True)
    return x * lax.rsqrt(ms + NORM_EPS) * g


def _dot(a, b):
    return jnp.dot(a, b, preferred_element_type=F32)


def _dot_nt(a, b):
    return lax.dot_general(a, b, (((1,), (1,)), ((), ())), preferred_element_type=F32)


def _mem_kv_kernel(mem_ref, g_ref, w_ref, kg_ref, kt_ref, v_ref):
    mem_n = _rms(mem_ref[0], g_ref[...]).astype(BF16)
    kv = _dot(mem_n, w_ref[...])
    width = kt_ref.shape[1]
    for h in range(MEM_HEADS):
        sl = slice(h * MEM_HEAD_DIM, (h + 1) * MEM_HEAD_DIM)
        kn = _rms(kv[:, sl], kg_ref[...])
        kt_ref[0, sl, :] = kn.T.astype(BF16)
    v_ref[0] = kv[:, width:].astype(BF16)


def _mem_kv(mem, g, w_kv, kg):
    b, m, d = mem.shape
    width = w_kv.shape[1] // 2
    return pl.pallas_call(
        _mem_kv_kernel,
        grid=(b,),
        in_specs=[
            pl.BlockSpec((1, m, d), lambda i: (i, 0, 0)),
            pl.BlockSpec((1, d), lambda i: (0, 0)),
            pl.BlockSpec((d, 2 * width), lambda i: (0, 0)),
            pl.BlockSpec((1, MEM_HEAD_DIM), lambda i: (0, 0)),
        ],
        out_specs=[
            pl.BlockSpec((1, width, m), lambda i: (i, 0, 0)),
            pl.BlockSpec((1, m, width), lambda i: (i, 0, 0)),
        ],
        out_shape=[
            jax.ShapeDtypeStruct((b, width, m), BF16),
            jax.ShapeDtypeStruct((b, m, width), BF16),
        ],
        compiler_params=pltpu.CompilerParams(
            dimension_semantics=("parallel",), vmem_limit_bytes=VMEM_LIMIT),
        name="mem_kv",
    )(mem, g, w_kv, kg)


def _in_proj_kernel(x_ref, g_ref, w_ref, e_ref, gn_ref, cw_ref, bg_ref, o_ref, inner_scr,
                    *, tiles_per_seq):
    i = pl.program_id(0)
    tm, d = x_ref.shape

    @pl.when(i % tiles_per_seq == 0)
    def _():
        inner_scr[0:HALO, :] = jnp.zeros((HALO, d), F32)

    h = _rms(x_ref[...], g_ref[...]).astype(BF16)

    def proj(c):
        return _dot(h, w_ref[:, c * d:(c + 1) * d])

    def put(c, val):
        o_ref[:, c * d:(c + 1) * d] = val.astype(BF16)

    def gates():
        for t in range(3):
            put(COL_GATE + t, jax.nn.sigmoid(proj(W_GATE + t) + bg_ref[:, t * d:(t + 1) * d]))

    def conv():
        inner = proj(W_GC) * proj(W_XC)
        inner_scr[HALO:HALO + tm, :] = inner
        taps = (cw_ref[2:3, :] * inner
                + cw_ref[1:2, :] * inner_scr[HALO - 1:HALO - 1 + tm, :]
                + cw_ref[0:1, :] * inner_scr[HALO - 2:HALO - 2 + tm, :])
        put(COL_YC, proj(W_GB) * taps)
        inner_scr[0:HALO, :] = inner_scr[tm:tm + HALO, :]

    def group_normed(slot, src, dst):
        acc = proj(src)
        sq = (acc * acc).astype(BF16)
        for c in range(d // MXU_EDGE):
            sl = slice(c * MXU_EDGE, (c + 1) * MXU_EDGE)
            ms = _dot(sq[:, sl], e_ref[slot])
            y = acc[:, sl] * lax.rsqrt(ms + NORM_EPS) * gn_ref[slot, :, sl]
            o_ref[:, dst * d + c * MXU_EDGE:dst * d + (c + 1) * MXU_EDGE] = y.astype(BF16)

    put(COL_V, proj(W_V))
    group_normed(2, W_QM, COL_QM)
    group_normed(1, W_K, COL_K)
    group_normed(0, W_Q, COL_Q)
    conv()
    gates()


def _resident(shape):
    return pl.BlockSpec(shape, lambda i: (0,) * len(shape), pipeline_mode=pl.Buffered(1))


def _in_proj(x2d, g, w_in, e_mats, gn, conv_w, b_gate, tm, seq):
    m, d = x2d.shape
    return pl.pallas_call(
        functools.partial(_in_proj_kernel, tiles_per_seq=seq // tm),
        grid=(m // tm,),
        in_specs=[
            pl.BlockSpec((tm, d), lambda i: (i, 0)),
            _resident((1, d)), _resident(w_in.shape), _resident(e_mats.shape),
            _resident(gn.shape), _resident(conv_w.shape), _resident(b_gate.shape),
        ],
        out_specs=pl.BlockSpec((tm, N_COL_BLOCKS * d), lambda i: (i, 0)),
        out_shape=jax.ShapeDtypeStruct((m, N_COL_BLOCKS * d), BF16),
        scratch_shapes=[pltpu.VMEM((HALO + tm, d), F32)],
        compiler_params=pltpu.CompilerParams(
            dimension_semantics=("arbitrary",), vmem_limit_bytes=VMEM_LIMIT),
        name="in_proj",
    )(x2d, g, w_in, e_mats, gn, conv_w, b_gate)


def _stack_maps(q):
    lane = lax.broadcasted_iota(jnp.int32, q.shape, 1)
    zero = jnp.zeros_like(q)
    return jnp.concatenate([jnp.where(lane < DIFF_HEAD_DIM, q, zero),
                            jnp.where(lane >= DIFF_HEAD_DIM, q, zero)], axis=0)


def _causal_mask(s, n):
    row = lax.broadcasted_iota(jnp.int32, (n, n), 0)
    col = lax.broadcasted_iota(jnp.int32, (n, n), 1)
    keep = jnp.concatenate([col <= row] * (s.shape[0] // n), axis=0)
    return jnp.where(keep, s, MASK_VALUE)


def _diff_lambda(lq1_ref, lk1_ref, lq2_ref, lk2_ref, lam_init):
    return (jnp.exp(jnp.sum(lq1_ref[...] * lk1_ref[...], axis=-1, keepdims=True))
            - jnp.exp(jnp.sum(lq2_ref[...] * lk2_ref[...], axis=-1, keepdims=True))
            + lam_init)


def _diff_finalize(acc, l, lam, sg, lam_init):
    n = acc.shape[0] // 2
    o = acc[:n] / l[:n] - lam * (acc[n:] / l[n:])
    return (_rms(o, sg) * (1.0 - lam_init)).astype(BF16)


def _diff_attn_online_kernel(lq1_ref, lk1_ref, lq2_ref, lk2_ref, sg_ref, q_ref, k_ref, v_ref,
                             o_ref, *, tq, lam_init):
    i = pl.program_id(2)
    qs = _stack_maps(q_ref[0])

    def step(j, carry, masked):
        m, l, acc = carry
        start = pl.multiple_of(j * tq, tq)
        s = _dot_nt(qs, k_ref[0, pl.ds(start, tq), :])
        if masked:
            s = _causal_mask(s, tq)
        m_new = jnp.maximum(m, jnp.max(s, axis=-1, keepdims=True))
        alpha = jnp.exp2(m - m_new)
        p = jnp.exp2(s - m_new)
        l = alpha * l + jnp.sum(p, axis=-1, keepdims=True)
        acc = alpha * acc + _dot(p.astype(BF16), v_ref[0, pl.ds(start, tq), :])
        return m_new, l, acc

    init = (jnp.full((2 * tq, 1), MASK_VALUE, F32), jnp.zeros((2 * tq, 1), F32),
            jnp.zeros((2 * tq, DIFF_V_DIM), F32))
    carry = lax.fori_loop(0, i, functools.partial(step, masked=False), init)
    _, l, acc = step(i, carry, masked=True)
    lam = _diff_lambda(lq1_ref, lk1_ref, lq2_ref, lk2_ref, lam_init)
    o_ref[0] = _diff_finalize(acc, l, lam, sg_ref[...], lam_init)


def _diff_attn_plain_kernel(lq1_ref, lk1_ref, lq2_ref, lk2_ref, sg_ref, q_ref, k_ref, v_ref,
                            o_ref, *, tq, lam_init, sub=MXU_EDGE):
    seq = q_ref.shape[1]
    lam = _diff_lambda(lq1_ref, lk1_ref, lq2_ref, lk2_ref, lam_init)

    def pv(s, start, size):
        v = v_ref[0, start:start + size, :]
        v_ones = jnp.concatenate([v, jnp.ones_like(v)], axis=1)
        return _dot(jnp.exp2(s).astype(BF16), v_ones)

    for i in reversed(range(seq // tq)):
        r0 = i * tq
        qs = _stack_maps(q_ref[0, r0:r0 + tq, :])
        acc = None
        for c in range(tq // sub):
            n = tq - c * sub
            qs_c = qs if c == 0 else jnp.concatenate([qs[c * sub:tq], qs[tq + c * sub:]], axis=0)
            s = _dot_nt(qs_c, k_ref[0, r0 + c * sub:r0 + (c + 1) * sub, :])
            pieces = [_causal_mask(s[:sub], sub), s[sub:n],
                      _causal_mask(s[n:n + sub], sub), s[n + sub:]]
            s = jnp.concatenate([p for p in pieces if p.shape[0]], axis=0)
            part = pv(s, r0 + c * sub, sub)
            if c == 0:
                acc = part
            else:
                acc = jnp.concatenate([acc[:c * sub], acc[c * sub:tq] + part[:n],
                                       acc[tq:tq + c * sub], acc[tq + c * sub:] + part[n:]],
                                      axis=0)
        for j in range(i):
            acc = acc + pv(_dot_nt(qs, k_ref[0, j * tq:(j + 1) * tq, :]), j * tq, tq)
        o_ref[0, r0:r0 + tq, :] = _diff_finalize(acc[:, :DIFF_V_DIM], acc[:, DIFF_V_DIM:], lam,
                                                 sg_ref[...], lam_init)


def _diff_attn(proj3d, lq1, lk1, lq2, lk2, subln_g, lam_init, tq, online):
    b, s, _ = proj3d.shape
    rows = tq if online else s
    grid = (b, DIFF_HEADS, s // rows)
    vec = pl.BlockSpec((1, DIFF_HEAD_DIM), lambda bi, h, i: (0, 0))
    body = _diff_attn_online_kernel if online else _diff_attn_plain_kernel
    return pl.pallas_call(
        functools.partial(body, tq=tq, lam_init=lam_init),
        grid=grid,
        in_specs=[
            vec, vec, vec, vec,
            pl.BlockSpec((1, DIFF_V_DIM), lambda bi, h, i: (0, 0)),
            pl.BlockSpec((1, rows, DIFF_V_DIM), lambda bi, h, i: (bi, i, COL_Q * DIFF_HEADS + h)),
            pl.BlockSpec((1, s, DIFF_V_DIM), lambda bi, h, i: (bi, 0, COL_K * DIFF_HEADS + h)),
            pl.BlockSpec((1, s, DIFF_V_DIM), lambda bi, h, i: (bi, 0, COL_V * DIFF_HEADS + h)),
        ],
        out_specs=pl.BlockSpec((1, rows, DIFF_V_DIM), lambda bi, h, i: (bi, i, h)),
        out_shape=jax.ShapeDtypeStruct((b, s, DIFF_HEADS * DIFF_V_DIM), BF16),
        compiler_params=pltpu.CompilerParams(
            dimension_semantics=("parallel", "parallel", "parallel"),
            vmem_limit_bytes=VMEM_LIMIT),
        name="diff_attn_online" if online else "diff_attn",
    )(lq1, lk1, lq2, lk2, subln_g, proj3d, proj3d, proj3d)


def _merge_kernel(x_ref, on_ref, yc_ref, qm_ref, ga_ref, gc_ref, gm_ref, kt_ref, vm_ref,
                  wa_ref, wc_ref, wm_ref, wo_ref, o_ref):
    ya = _dot(on_ref[...], wa_ref[...])
    yc = _dot(yc_ref[...], wc_ref[...])

    heads = []
    for h in range(MEM_HEADS):
        sl = slice(h * MEM_HEAD_DIM, (h + 1) * MEM_HEAD_DIM)
        s = _dot(qm_ref[:, sl], kt_ref[0, sl, :])
        p = jnp.exp(s - jnp.max(s, axis=-1, keepdims=True))
        p = p / jnp.sum(p, axis=-1, keepdims=True)
        heads.append(_dot(p.astype(BF16), vm_ref[0, :, sl]).astype(BF16))
    ym = _dot(jnp.concatenate(heads, axis=-1), wm_ref[...])

    merged = (ga_ref[...].astype(F32) * ya + gc_ref[...].astype(F32) * yc
              + gm_ref[...].astype(F32) * ym)
    o_ref[...] = x_ref[...] + _dot(merged.astype(BF16), wo_ref[...])


def _merge(x2d, on2d, proj, kt, vm, wa, wc, wm, wo, tm, seq):
    m, d = x2d.shape
    tiles_per_seq = seq // tm

    def col(c):
        return pl.BlockSpec((tm, d), lambda i: (i, c))

    mem_len = vm.shape[1]
    return pl.pallas_call(
        _merge_kernel,
        grid=(m // tm,),
        in_specs=[
            col(0), col(0),
            col(COL_YC), col(COL_QM), col(COL_GATE), col(COL_GATE + 1), col(COL_GATE + 2),
            pl.BlockSpec((1, d, mem_len), lambda i: (i // tiles_per_seq, 0, 0)),
            pl.BlockSpec((1, mem_len, d), lambda i: (i // tiles_per_seq, 0, 0)),
            _resident((d, d)), _resident((d, d)), _resident((d, d)), _resident((d, d)),
        ],
        out_specs=col(0),
        out_shape=jax.ShapeDtypeStruct((m, d), F32),
        compiler_params=pltpu.CompilerParams(
            dimension_semantics=("parallel",), vmem_limit_bytes=VMEM_LIMIT),
        name="merge",
    )(x2d, on2d, proj, proj, proj, proj, proj, kt, vm, wa, wc, wm, wo)


def _mlp_kernel(x_ref, g_ref, w1_ref, w2_ref, o_ref, *, tf):
    x = x_ref[...]
    h = _rms(x, g_ref[...]).astype(BF16)
    acc = x
    for f in range(w1_ref.shape[1] // tf):
        u = jnp.square(jnp.maximum(_dot(h, w1_ref[:, f * tf:(f + 1) * tf]), 0.0))
        acc = acc + _dot(u.astype(BF16), w2_ref[f * tf:(f + 1) * tf, :])
    o_ref[...] = acc


def _mlp(x2d, g, w1, w2, tm, tf):
    m, d = x2d.shape
    return pl.pallas_call(
        functools.partial(_mlp_kernel, tf=tf),
        grid=(m // tm,),
        in_specs=[
            pl.BlockSpec((tm, d), lambda i: (i, 0)),
            _resident((1, d)), _resident(w1.shape), _resident(w2.shape),
        ],
        out_specs=pl.BlockSpec((tm, d), lambda i: (i, 0)),
        out_shape=jax.ShapeDtypeStruct((m, d), F32),
        compiler_params=pltpu.CompilerParams(
            dimension_semantics=("parallel",), vmem_limit_bytes=VMEM_LIMIT),
        name="mlp",
    )(x2d, g, w1, w2)


def _group_mean_matrix(group):
    idx = np.arange(MXU_EDGE) // group
    return (idx[:, None] == idx[None, :]).astype(np.float32) / group


def _layer(x, mem, lam_init, norm_mix_g, norm_mem_g, w_in, b_gate, q_norm_g, k_norm_g,
           lam_q1, lam_k1, lam_q2, lam_k2, subln_g, w_attn_o, conv_w, w_conv_o, w_mem_kv,
           mq_norm_g, mk_norm_g, w_mem_o, w_o, norm_mlp_g, w_mlp_in, w_mlp_out):
    b, s, d = x.shape
    x2d = x.reshape(b * s, d)
    row = lambda v: v.reshape(1, -1)

    e_mats = jnp.asarray(np.stack([_group_mean_matrix(DIFF_HEAD_DIM),
                                   _group_mean_matrix(DIFF_HEAD_DIM),
                                   _group_mean_matrix(MEM_HEAD_DIM)]), BF16)
    gn = jnp.stack([
        jnp.tile(q_norm_g * (LOG2_E * DIFF_HEAD_DIM ** -0.5), d // DIFF_HEAD_DIM),
        jnp.tile(k_norm_g, d // DIFF_HEAD_DIM),
        jnp.tile(mq_norm_g * MEM_HEAD_DIM ** -0.5, d // MEM_HEAD_DIM),
    ]).reshape(3, 1, d)

    kt, vm = _mem_kv(mem, row(norm_mem_g), w_mem_kv.astype(BF16), row(mk_norm_g))
    proj = _in_proj(x2d, row(norm_mix_g), w_in.astype(BF16), e_mats, gn, conv_w, row(b_gate),
                    tm=512, seq=s)

    score_bound = DIFF_HEAD_DIM ** 0.5 * jnp.max(jnp.abs(q_norm_g)) * jnp.max(jnp.abs(k_norm_g))
    attn = functools.partial(_diff_attn, proj.reshape(b, s, -1), row(lam_q1), row(lam_k1),
                             row(lam_q2), row(lam_k2), row(subln_g), lam_init)
    o_n = lax.cond(score_bound <= MAX_UNSHIFTED_LOGIT,
                   lambda: attn(tq=1024, online=False), lambda: attn(tq=512, online=True))
    x1 = _merge(x2d, o_n.reshape(b * s, d), proj, kt, vm,
                w_attn_o.astype(BF16), w_conv_o.astype(BF16), w_mem_o.astype(BF16),
                w_o.astype(BF16), tm=512, seq=s)
    x2 = _mlp(x1, row(norm_mlp_g), w_mlp_in.astype(BF16), w_mlp_out.astype(BF16), tm=1024, tf=1024)
    return x2.reshape(b, s, d)


def kernel(x, mem, norm_mix_g, norm_mem_g, w_in, b_gate, q_norm_g, k_norm_g, lam_q1, lam_k1,
           lam_q2, lam_k2, subln_g, w_attn_o, conv_w, w_conv_o, w_mem_kv, mq_norm_g, mk_norm_g,
           w_mem_o, w_o, norm_mlp_g, w_mlp_in, w_mlp_out):
    depth = w_in.shape[0]
    for layer in range(depth):
        lam_init = 0.8 - 0.6 * float(np.exp(-0.3 * layer))
        x = _layer(x, mem, lam_init, norm_mix_g[layer], norm_mem_g[layer], w_in[layer],
                   b_gate[layer], q_norm_g[layer], k_norm_g[layer], lam_q1[layer], lam_k1[layer],
                   lam_q2[layer], lam_k2[layer], subln_g[layer], w_attn_o[layer], conv_w[layer],
                   w_conv_o[layer], w_mem_kv[layer], mq_norm_g[layer], mk_norm_g[layer],
                   w_mem_o[layer], w_o[layer], norm_mlp_g[layer], w_mlp_in[layer],
                   w_mlp_out[layer])
    return x
```

```python
import functools

import jax
import jax.numpy as jnp
import numpy as np
from jax import lax
from jax.experimental import pallas as pl
from jax.experimental.pallas import tpu as pltpu

F32 = jnp.float32
BF16 = jnp.bfloat16

NORM_EPS = 1e-6
MASK_VALUE = -1e30
LOG2_E = 1.4426950408889634
MAX_UNSHIFTED_LOGIT = 60.0

DIFF_HEADS = 8
DIFF_HEAD_DIM = 64
DIFF_V_DIM = 2 * DIFF_HEAD_DIM
MEM_HEADS = 4
MEM_HEAD_DIM = 256
CONV_K = 3

MXU_EDGE = 256
HALO = 8

W_Q, W_K, W_V, W_XC, W_GB, W_GC, W_QM, W_GATE = 0, 1, 2, 3, 4, 5, 6, 7
COL_Q, COL_K, COL_V, COL_YC, COL_QM, COL_GATE = 0, 1, 2, 3, 4, 5
N_COL_BLOCKS = 8

VMEM_LIMIT = 56 * 1024 * 1024


def _rms(x, g):
    ms = jnp.mean(x * x, axis=-1, keepdims=True)
    return x * lax.rsqrt(ms + NORM_EPS) * g


def _dot(a, b):
    return jnp.dot(a, b, preferred_element_type=F32)


def _dot_nt(a, b):
    return lax.dot_general(a, b, (((1,), (1,)), ((), ())), preferred_element_type=F32)


def _mem_kv_kernel(mem_ref, g_ref, w_ref, kg_ref, kt_ref, v_ref):
    mem_n = _rms(mem_ref[0], g_ref[...]).astype(BF16)
    kv = _dot(mem_n, w_ref[...])
    width = kt_ref.shape[1]
    for h in range(MEM_HEADS):
        sl = slice(h * MEM_HEAD_DIM, (h + 1) * MEM_HEAD_DIM)
        kn = _rms(kv[:, sl], kg_ref[...])
        kt_ref[0, sl, :] = kn.T.astype(BF16)
    v_ref[0] = kv[:, width:].astype(BF16)


def _mem_kv(mem, g, w_kv, kg):
    b, m, d = mem.shape
    width = w_kv.shape[1] // 2
    return pl.pallas_call(
        _mem_kv_kernel,
        grid=(b,),
        in_specs=[
            pl.BlockSpec((1, m, d), lambda i: (i, 0, 0)),
            pl.BlockSpec((1, d), lambda i: (0, 0)),
            pl.BlockSpec((d, 2 * width), lambda i: (0, 0)),
            pl.BlockSpec((1, MEM_HEAD_DIM), lambda i: (0, 0)),
        ],
        out_specs=[
            pl.BlockSpec((1, width, m), lambda i: (i, 0, 0)),
            pl.BlockSpec((1, m, width), lambda i: (i, 0, 0)),
        ],
        out_shape=[
            jax.ShapeDtypeStruct((b, width, m), BF16),
            jax.ShapeDtypeStruct((b, m, width), BF16),
        ],
        compiler_params=pltpu.CompilerParams(
            dimension_semantics=("parallel",), vmem_limit_bytes=VMEM_LIMIT),
        name="mem_kv",
    )(mem, g, w_kv, kg)


def _in_proj_kernel(x_ref, g_ref, w_ref, e_ref, gn_ref, cw_ref, bg_ref, o_ref, inner_scr,
                    *, tiles_per_seq):
    i = pl.program_id(0)
    tm, d = x_ref.shape

    @pl.when(i % tiles_per_seq == 0)
    def _():
        inner_scr[0:HALO, :] = jnp.zeros((HALO, d), F32)

    h = _rms(x_ref[...], g_ref[...]).astype(BF16)

    def proj(c):
        return _dot(h, w_ref[:, c * d:(c + 1) * d])

    def put(c, val):
        o_ref[:, c * d:(c + 1) * d] = val.astype(BF16)

    def gates():
        for t in range(3):
            z = proj(W_GATE + t) + bg_ref[:, t * d:(t + 1) * d]
            put(COL_GATE + t, 0.5 + 0.5 * jnp.tanh(0.5 * z))

    def conv():
        inner = proj(W_GC) * proj(W_XC)
        inner_scr[HALO:HALO + tm, :] = inner
        taps = (cw_ref[2:3, :] * inner
                + cw_ref[1:2, :] * inner_scr[HALO - 1:HALO - 1 + tm, :]
                + cw_ref[0:1, :] * inner_scr[HALO - 2:HALO - 2 + tm, :])
        put(COL_YC, proj(W_GB) * taps)
        inner_scr[0:HALO, :] = inner_scr[tm:tm + HALO, :]

    def group_normed(slot, src, dst):
        acc = proj(src)
        sq = (acc * acc).astype(BF16)
        for c in range(d // MXU_EDGE):
            sl = slice(c * MXU_EDGE, (c + 1) * MXU_EDGE)
            ms = _dot(sq[:, sl], e_ref[...])
            y = acc[:, sl] * lax.rsqrt(ms + NORM_EPS) * gn_ref[slot, :, sl]
            o_ref[:, dst * d + c * MXU_EDGE:dst * d + (c + 1) * MXU_EDGE] = y.astype(BF16)

    def head_normed(slot, src, dst, width):
        acc = proj(src)
        for c in range(d // width):
            sl = slice(c * width, (c + 1) * width)
            o_ref[:, dst * d + c * width:dst * d + (c + 1) * width] = _rms(
                acc[:, sl], gn_ref[slot, :, sl]).astype(BF16)

    put(COL_V, proj(W_V))
    head_normed(2, W_QM, COL_QM, MEM_HEAD_DIM)
    group_normed(1, W_K, COL_K)
    group_normed(0, W_Q, COL_Q)
    conv()
    gates()


def _resident(shape):
    return pl.BlockSpec(shape, lambda i: (0,) * len(shape), pipeline_mode=pl.Buffered(1))


def _in_proj(x2d, g, w_in, e_mats, gn, conv_w, b_gate, tm, seq):
    m, d = x2d.shape
    return pl.pallas_call(
        functools.partial(_in_proj_kernel, tiles_per_seq=seq // tm),
        grid=(m // tm,),
        in_specs=[
            pl.BlockSpec((tm, d), lambda i: (i, 0)),
            _resident((1, d)), _resident(w_in.shape), _resident(e_mats.shape),
            _resident(gn.shape), _resident(conv_w.shape), _resident(b_gate.shape),
        ],
        out_specs=pl.BlockSpec((tm, N_COL_BLOCKS * d), lambda i: (i, 0)),
        out_shape=jax.ShapeDtypeStruct((m, N_COL_BLOCKS * d), BF16),
        scratch_shapes=[pltpu.VMEM((HALO + tm, d), F32)],
        compiler_params=pltpu.CompilerParams(
            dimension_semantics=("arbitrary",), vmem_limit_bytes=VMEM_LIMIT),
        name="in_proj",
    )(x2d, g, w_in, e_mats, gn, conv_w, b_gate)


def _stack_maps(q):
    lane = lax.broadcasted_iota(jnp.int32, q.shape, 1)
    zero = jnp.zeros_like(q)
    return jnp.concatenate([jnp.where(lane < DIFF_HEAD_DIM, q, zero),
                            jnp.where(lane >= DIFF_HEAD_DIM, q, zero)], axis=0)


def _causal_mask(s, n):
    row = lax.broadcasted_iota(jnp.int32, (n, n), 0)
    col = lax.broadcasted_iota(jnp.int32, (n, n), 1)
    keep = jnp.concatenate([col <= row] * (s.shape[0] // n), axis=0)
    return jnp.where(keep, s, MASK_VALUE)


def _diff_lambda(lq1_ref, lk1_ref, lq2_ref, lk2_ref, lam_init):
    return (jnp.exp(jnp.sum(lq1_ref[...] * lk1_ref[...], axis=-1, keepdims=True))
            - jnp.exp(jnp.sum(lq2_ref[...] * lk2_ref[...], axis=-1, keepdims=True))
            + lam_init)


def _diff_finalize(acc, l, lam, sg, lam_init):
    n = acc.shape[0] // 2
    o = acc[:n] / l[:n] - lam * (acc[n:] / l[n:])
    return (_rms(o, sg) * (1.0 - lam_init)).astype(BF16)


def _diff_attn_online_kernel(lq1_ref, lk1_ref, lq2_ref, lk2_ref, sg_ref, q_ref, k_ref, v_ref,
                             o_ref, *, tq, lam_init):
    i = pl.program_id(2)
    qs = _stack_maps(q_ref[0])

    def step(j, carry, masked):
        m, l, acc = carry
        start = pl.multiple_of(j * tq, tq)
        s = _dot_nt(qs, k_ref[0, pl.ds(start, tq), :])
        if masked:
            s = _causal_mask(s, tq)
        m_new = jnp.maximum(m, jnp.max(s, axis=-1, keepdims=True))
        alpha = jnp.exp2(m - m_new)
        p = jnp.exp2(s - m_new)
        l = alpha * l + jnp.sum(p, axis=-1, keepdims=True)
        acc = alpha * acc + _dot(p.astype(BF16), v_ref[0, pl.ds(start, tq), :])
        return m_new, l, acc

    init = (jnp.full((2 * tq, 1), MASK_VALUE, F32), jnp.zeros((2 * tq, 1), F32),
            jnp.zeros((2 * tq, DIFF_V_DIM), F32))
    carry = lax.fori_loop(0, i, functools.partial(step, masked=False), init)
    _, l, acc = step(i, carry, masked=True)
    lam = _diff_lambda(lq1_ref, lk1_ref, lq2_ref, lk2_ref, lam_init)
    o_ref[0] = _diff_finalize(acc, l, lam, sg_ref[...], lam_init)


def _diff_attn_plain_kernel(lq1_ref, lk1_ref, lq2_ref, lk2_ref, sg_ref, q_ref, k_ref, v_ref,
                            o_ref, *, tq, lam_init, sub=MXU_EDGE):
    seq = q_ref.shape[1]
    lam = _diff_lambda(lq1_ref, lk1_ref, lq2_ref, lk2_ref, lam_init)

    def pv(s, start, size):
        v = v_ref[0, start:start + size, :]
        v_ones = jnp.concatenate([v, jnp.ones_like(v)], axis=1)
        return _dot(jnp.exp2(s).astype(BF16), v_ones)

    for i in reversed(range(seq // tq)):
        r0 = i * tq
        qs = _stack_maps(q_ref[0, r0:r0 + tq, :])
        acc = None
        for c in range(tq // sub):
            n = tq - c * sub
            qs_c = qs if c == 0 else jnp.concatenate([qs[c * sub:tq], qs[tq + c * sub:]], axis=0)
            s = _dot_nt(qs_c, k_ref[0, r0 + c * sub:r0 + (c + 1) * sub, :])
            pieces = [_causal_mask(s[:sub], sub), s[sub:n],
                      _causal_mask(s[n:n + sub], sub), s[n + sub:]]
            s = jnp.concatenate([p for p in pieces if p.shape[0]], axis=0)
            part = pv(s, r0 + c * sub, sub)
            if c == 0:
                acc = part
            else:
                acc = jnp.concatenate([acc[:c * sub], acc[c * sub:tq] + part[:n],
                                       acc[tq:tq + c * sub], acc[tq + c * sub:] + part[n:]],
                                      axis=0)
        for j in range(i):
            acc = acc + pv(_dot_nt(qs, k_ref[0, j * tq:(j + 1) * tq, :]), j * tq, tq)
        o_ref[0, r0:r0 + tq, :] = _diff_finalize(acc[:, :DIFF_V_DIM], acc[:, DIFF_V_DIM:], lam,
                                                 sg_ref[...], lam_init)


def _diff_attn(proj3d, lq1, lk1, lq2, lk2, subln_g, lam_init, tq, online):
    b, s, _ = proj3d.shape
    rows = tq if online else s
    grid = (b, DIFF_HEADS, s // rows)
    vec = pl.BlockSpec((1, DIFF_HEAD_DIM), lambda bi, h, i: (0, 0))
    body = _diff_attn_online_kernel if online else _diff_attn_plain_kernel
    return pl.pallas_call(
        functools.partial(body, tq=tq, lam_init=lam_init),
        grid=grid,
        in_specs=[
            vec, vec, vec, vec,
            pl.BlockSpec((1, DIFF_V_DIM), lambda bi, h, i: (0, 0)),
            pl.BlockSpec((1, rows, DIFF_V_DIM), lambda bi, h, i: (bi, i, COL_Q * DIFF_HEADS + h)),
            pl.BlockSpec((1, s, DIFF_V_DIM), lambda bi, h, i: (bi, 0, COL_K * DIFF_HEADS + h)),
            pl.BlockSpec((1, s, DIFF_V_DIM), lambda bi, h, i: (bi, 0, COL_V * DIFF_HEADS + h)),
        ],
        out_specs=pl.BlockSpec((1, rows, DIFF_V_DIM), lambda bi, h, i: (bi, i, h)),
        out_shape=jax.ShapeDtypeStruct((b, s, DIFF_HEADS * DIFF_V_DIM), BF16),
        compiler_params=pltpu.CompilerParams(
            dimension_semantics=("parallel", "parallel", "parallel"),
            vmem_limit_bytes=VMEM_LIMIT),
        name="diff_attn_online" if online else "diff_attn",
    )(lq1, lk1, lq2, lk2, subln_g, proj3d, proj3d, proj3d)


def _merge_kernel(x_ref, on_ref, yc_ref, qm_ref, ga_ref, gc_ref, gm_ref, kt_ref, vm_ref,
                  wa_ref, wc_ref, wm_ref, wo_ref, o_ref):
    ya = _dot(on_ref[...], wa_ref[...])
    yc = _dot(yc_ref[...], wc_ref[...])

    heads = []
    for h in range(MEM_HEADS):
        sl = slice(h * MEM_HEAD_DIM, (h + 1) * MEM_HEAD_DIM)
        s = _dot(qm_ref[:, sl], kt_ref[0, sl, :])
        p = jnp.exp(s - jnp.max(s, axis=-1, keepdims=True))
        p = p / jnp.sum(p, axis=-1, keepdims=True)
        heads.append(_dot(p.astype(BF16), vm_ref[0, :, sl]).astype(BF16))
    ym = _dot(jnp.concatenate(heads, axis=-1), wm_ref[...])

    merged = (ga_ref[...].astype(F32) * ya + gc_ref[...].astype(F32) * yc
              + gm_ref[...].astype(F32) * ym)
    o_ref[...] = x_ref[...] + _dot(merged.astype(BF16), wo_ref[...])


def _merge(x2d, on2d, proj, kt, vm, wa, wc, wm, wo, tm, seq):
    m, d = x2d.shape
    tiles_per_seq = seq // tm

    def col(c):
        return pl.BlockSpec((tm, d), lambda i: (i, c))

    mem_len = vm.shape[1]
    return pl.pallas_call(
        _merge_kernel,
        grid=(m // tm,),
        in_specs=[
            col(0), col(0),
            col(COL_YC), col(COL_QM), col(COL_GATE), col(COL_GATE + 1), col(COL_GATE + 2),
            pl.BlockSpec((1, d, mem_len), lambda i: (i // tiles_per_seq, 0, 0)),
            pl.BlockSpec((1, mem_len, d), lambda i: (i // tiles_per_seq, 0, 0)),
            _resident((d, d)), _resident((d, d)), _resident((d, d)), _resident((d, d)),
        ],
        out_specs=col(0),
        out_shape=jax.ShapeDtypeStruct((m, d), F32),
        compiler_params=pltpu.CompilerParams(
            dimension_semantics=("parallel",), vmem_limit_bytes=VMEM_LIMIT),
        name="merge",
    )(x2d, on2d, proj, proj, proj, proj, proj, kt, vm, wa, wc, wm, wo)


def _mlp_kernel(x_ref, g_ref, w1_ref, w2_ref, o_ref, *, tf):
    x = x_ref[...]
    h = _rms(x, g_ref[...]).astype(BF16)
    acc = x
    for f in range(w1_ref.shape[1] // tf):
        u = jnp.square(jnp.maximum(_dot(h, w1_ref[:, f * tf:(f + 1) * tf]), 0.0))
        acc = acc + _dot(u.astype(BF16), w2_ref[f * tf:(f + 1) * tf, :])
    o_ref[...] = acc


def _mlp(x2d, g, w1, w2, tm, tf):
    m, d = x2d.shape
    return pl.pallas_call(
        functools.partial(_mlp_kernel, tf=tf),
        grid=(m // tm,),
        in_specs=[
            pl.BlockSpec((tm, d), lambda i: (i, 0)),
            _resident((1, d)), _resident(w1.shape), _resident(w2.shape),
        ],
        out_specs=pl.BlockSpec((tm, d), lambda i: (i, 0)),
        out_shape=jax.ShapeDtypeStruct((m, d), F32),
        compiler_params=pltpu.CompilerParams(
            dimension_semantics=("parallel",), vmem_limit_bytes=VMEM_LIMIT),
        name="mlp",
    )(x2d, g, w1, w2)


def _group_mean_matrix(group):
    idx = np.arange(MXU_EDGE) // group
    return (idx[:, None] == idx[None, :]).astype(np.float32) / group


def _layer(x, mem, lam_init, norm_mix_g, norm_mem_g, w_in, b_gate, q_norm_g, k_norm_g,
           lam_q1, lam_k1, lam_q2, lam_k2, subln_g, w_attn_o, conv_w, w_conv_o, w_mem_kv,
           mq_norm_g, mk_norm_g, w_mem_o, w_o, norm_mlp_g, w_mlp_in, w_mlp_out):
    b, s, d = x.shape
    x2d = x.reshape(b * s, d)
    row = lambda v: v.reshape(1, -1)

    e_mats = jnp.asarray(_group_mean_matrix(DIFF_HEAD_DIM), BF16)
    gn = jnp.stack([
        jnp.tile(q_norm_g * (LOG2_E * DIFF_HEAD_DIM ** -0.5), d // DIFF_HEAD_DIM),
        jnp.tile(k_norm_g, d // DIFF_HEAD_DIM),
        jnp.tile(mq_norm_g * MEM_HEAD_DIM ** -0.5, d // MEM_HEAD_DIM),
    ]).reshape(3, 1, d)

    kt, vm = _mem_kv(mem, row(norm_mem_g), w_mem_kv.astype(BF16), row(mk_norm_g))
    proj = _in_proj(x2d, row(norm_mix_g), w_in.astype(BF16), e_mats, gn, conv_w, row(b_gate),
                    tm=512, seq=s)

    score_bound = DIFF_HEAD_DIM ** 0.5 * jnp.max(jnp.abs(q_norm_g)) * jnp.max(jnp.abs(k_norm_g))
    attn = functools.partial(_diff_attn, proj.reshape(b, s, -1), row(lam_q1), row(lam_k1),
                             row(lam_q2), row(lam_k2), row(subln_g), lam_init)
    o_n = lax.cond(score_bound <= MAX_UNSHIFTED_LOGIT,
                   lambda: attn(tq=1024, online=False), lambda: attn(tq=512, online=True))
    x1 = _merge(x2d, o_n.reshape(b * s, d), proj, kt, vm,
                w_attn_o.astype(BF16), w_conv_o.astype(BF16), w_mem_o.astype(BF16),
                w_o.astype(BF16), tm=512, seq=s)
    x2 = _mlp(x1, row(norm_mlp_g), w_mlp_in.astype(BF16), w_mlp_out.astype(BF16), tm=1024, tf=1024)
    return x2.reshape(b, s, d)


def kernel(x, mem, norm_mix_g, norm_mem_g, w_in, b_gate, q_norm_g, k_norm_g, lam_q1, lam_k1,
           lam_q2, lam_k2, subln_g, w_attn_o, conv_w, w_conv_o, w_mem_kv, mq_norm_g, mk_norm_g,
           w_mem_o, w_o, norm_mlp_g, w_mlp_in, w_mlp_out):
    depth = w_in.shape[0]
    for layer in range(depth):
        lam_init = 0.8 - 0.6 * float(np.exp(-0.3 * layer))
        x = _layer(x, mem, lam_init, norm_mix_g[layer], norm_mem_g[layer], w_in[layer],
                   b_gate[layer], q_norm_g[layer], k_norm_g[layer], lam_q1[layer], lam_k1[layer],
                   lam_q2[layer], lam_k2[layer], subln_g[layer], w_attn_o[layer], conv_w[layer],
                   w_conv_o[layer], w_mem_kv[layer], mq_norm_g[layer], mk_norm_g[layer],
                   w_mem_o[layer], w_o[layer], norm_mlp_g[layer], w_mlp_in[layer],
                   w_mlp_out[layer])
    return x
```

```python
import functools

import jax
import jax.numpy as jnp
import numpy as np
from jax import lax
from jax.experimental import pallas as pl
from jax.experimental.pallas import tpu as pltpu

F32 = jnp.float32
BF16 = jnp.bfloat16

NORM_EPS = 1e-6
MASK_VALUE = -1e30
LOG2_E = 1.4426950408889634
MAX_UNSHIFTED_LOGIT = 60.0

DIFF_HEADS = 8
DIFF_HEAD_DIM = 64
DIFF_V_DIM = 2 * DIFF_HEAD_DIM
MEM_HEADS = 4
MEM_HEAD_DIM = 256
CONV_K = 3

MXU_EDGE = 256
HALO = 8

W_Q, W_K, W_V, W_XC, W_GB, W_GC, W_QM, W_GATE = 0, 1, 2, 3, 4, 5, 6, 7
COL_Q, COL_K, COL_V, COL_YC, COL_QM, COL_GATE = 0, 1, 2, 3, 4, 5
N_COL_BLOCKS = 8

VMEM_LIMIT = 56 * 1024 * 1024


def _rms(x, g):
    ms = jnp.mean(x * x, axis=-1, keepdims=True)
    return x * lax.rsqrt(ms + NORM_EPS) * g


def _dot(a, b):
    return jnp.dot(a, b, preferred_element_type=F32)


def _dot_nt(a, b):
    return lax.dot_general(a, b, (((1,), (1,)), ((), ())), preferred_element_type=F32)


def _mem_kv_kernel(mem_ref, g_ref, w_ref, kg_ref, kt_ref, v_ref):
    mem_n = _rms(mem_ref[0], g_ref[...]).astype(BF16)
    kv = _dot(mem_n, w_ref[...])
    width = kt_ref.shape[1]
    for h in range(MEM_HEADS):
        sl = slice(h * MEM_HEAD_DIM, (h + 1) * MEM_HEAD_DIM)
        kn = _rms(kv[:, sl], kg_ref[...])
        kt_ref[0, sl, :] = kn.T.astype(BF16)
    v_ref[0] = kv[:, width:].astype(BF16)


def _mem_kv(mem, g, w_kv, kg):
    b, m, d = mem.shape
    width = w_kv.shape[1] // 2
    return pl.pallas_call(
        _mem_kv_kernel,
        grid=(b,),
        in_specs=[
            pl.BlockSpec((1, m, d), lambda i: (i, 0, 0)),
            pl.BlockSpec((1, d), lambda i: (0, 0)),
            pl.BlockSpec((d, 2 * width), lambda i: (0, 0)),
            pl.BlockSpec((1, MEM_HEAD_DIM), lambda i: (0, 0)),
        ],
        out_specs=[
            pl.BlockSpec((1, width, m), lambda i: (i, 0, 0)),
            pl.BlockSpec((1, m, width), lambda i: (i, 0, 0)),
        ],
        out_shape=[
            jax.ShapeDtypeStruct((b, width, m), BF16),
            jax.ShapeDtypeStruct((b, m, width), BF16),
        ],
        compiler_params=pltpu.CompilerParams(
            dimension_semantics=("parallel",), vmem_limit_bytes=VMEM_LIMIT),
        name="mem_kv",
    )(mem, g, w_kv, kg)


def _in_proj_kernel(x_ref, g_ref, w_ref, gn_ref, cw_ref, bg_ref, o_ref, inner_scr,
                    *, tiles_per_seq):
    i = pl.program_id(0)
    tm, d = x_ref.shape

    @pl.when(i % tiles_per_seq == 0)
    def _():
        inner_scr[0:HALO, :] = jnp.zeros((HALO, d), F32)

    h = _rms(x_ref[...], g_ref[...]).astype(BF16)

    def proj(c):
        return _dot(h, w_ref[:, c * d:(c + 1) * d])

    def put(c, val):
        o_ref[:, c * d:(c + 1) * d] = val.astype(BF16)

    def gates():
        for t in range(3):
            z = proj(W_GATE + t) + bg_ref[:, t * d:(t + 1) * d]
            put(COL_GATE + t, 0.5 + 0.5 * jnp.tanh(0.5 * z))

    def conv():
        inner = proj(W_GC) * proj(W_XC)
        inner_scr[HALO:HALO + tm, :] = inner
        taps = (cw_ref[2:3, :] * inner
                + cw_ref[1:2, :] * inner_scr[HALO - 1:HALO - 1 + tm, :]
                + cw_ref[0:1, :] * inner_scr[HALO - 2:HALO - 2 + tm, :])
        put(COL_YC, proj(W_GB) * taps)
        inner_scr[0:HALO, :] = inner_scr[tm:tm + HALO, :]

    def group_normed(slot, src, dst):
        acc = proj(src)
        lane = lax.broadcasted_iota(jnp.int32, (tm, 2 * DIFF_HEAD_DIM), 1)
        low = lane < DIFF_HEAD_DIM
        for c in range(d // (2 * DIFF_HEAD_DIM)):
            sl = slice(c * 2 * DIFF_HEAD_DIM, (c + 1) * 2 * DIFF_HEAD_DIM)
            a = acc[:, sl]
            sq = a * a
            s_lo = jnp.sum(jnp.where(low, sq, 0.0), axis=-1, keepdims=True)
            s_hi = jnp.sum(jnp.where(low, 0.0, sq), axis=-1, keepdims=True)
            ms = jnp.where(low, s_lo, s_hi) * (1.0 / DIFF_HEAD_DIM)
            y = a * lax.rsqrt(ms + NORM_EPS) * gn_ref[slot, :, sl]
            o_ref[:, dst * d + sl.start:dst * d + sl.stop] = y.astype(BF16)

    def head_normed(slot, src, dst, width):
        acc = proj(src)
        for c in range(d // width):
            sl = slice(c * width, (c + 1) * width)
            o_ref[:, dst * d + c * width:dst * d + (c + 1) * width] = _rms(
                acc[:, sl], gn_ref[slot, :, sl]).astype(BF16)

    group_normed(1, W_K, COL_K)
    group_normed(0, W_Q, COL_Q)
    gates()
    conv()
    head_normed(2, W_QM, COL_QM, MEM_HEAD_DIM)
    put(COL_V, proj(W_V))


def _resident(shape):
    return pl.BlockSpec(shape, lambda i: (0,) * len(shape), pipeline_mode=pl.Buffered(1))


def _in_proj(x2d, g, w_in, gn, conv_w, b_gate, tm, seq):
    m, d = x2d.shape
    return pl.pallas_call(
        functools.partial(_in_proj_kernel, tiles_per_seq=seq // tm),
        grid=(m // tm,),
        in_specs=[
            pl.BlockSpec((tm, d), lambda i: (i, 0)),
            _resident((1, d)), _resident(w_in.shape),
            _resident(gn.shape), _resident(conv_w.shape), _resident(b_gate.shape),
        ],
        out_specs=pl.BlockSpec((tm, N_COL_BLOCKS * d), lambda i: (i, 0)),
        out_shape=jax.ShapeDtypeStruct((m, N_COL_BLOCKS * d), BF16),
        scratch_shapes=[pltpu.VMEM((HALO + tm, d), F32)],
        compiler_params=pltpu.CompilerParams(
            dimension_semantics=("arbitrary",), vmem_limit_bytes=VMEM_LIMIT),
        name="in_proj",
    )(x2d, g, w_in, gn, conv_w, b_gate)


def _stack_maps(q):
    lane = lax.broadcasted_iota(jnp.int32, q.shape, 1)
    zero = jnp.zeros_like(q)
    return jnp.concatenate([jnp.where(lane < DIFF_HEAD_DIM, q, zero),
                            jnp.where(lane >= DIFF_HEAD_DIM, q, zero)], axis=0)


def _causal_mask(s, n):
    row = lax.broadcasted_iota(jnp.int32, (n, n), 0)
    col = lax.broadcasted_iota(jnp.int32, (n, n), 1)
    keep = jnp.concatenate([col <= row] * (s.shape[0] // n), axis=0)
    return jnp.where(keep, s, MASK_VALUE)


def _diff_lambda(lq1_ref, lk1_ref, lq2_ref, lk2_ref, lam_init):
    return (jnp.exp(jnp.sum(lq1_ref[...] * lk1_ref[...], axis=-1, keepdims=True))
            - jnp.exp(jnp.sum(lq2_ref[...] * lk2_ref[...], axis=-1, keepdims=True))
            + lam_init)


def _diff_finalize(acc, l, lam, sg, lam_init):
    n = acc.shape[0] // 2
    o = acc[:n] / l[:n] - lam * (acc[n:] / l[n:])
    return (_rms(o, sg) * (1.0 - lam_init)).astype(BF16)


def _diff_attn_online_kernel(lq1_ref, lk1_ref, lq2_ref, lk2_ref, sg_ref, q_ref, k_ref, v_ref,
                             o_ref, *, tq, lam_init):
    i = pl.program_id(2)
    qs = _stack_maps(q_ref[0])

    def step(j, carry, masked):
        m, l, acc = carry
        start = pl.multiple_of(j * tq, tq)
        s = _dot_nt(qs, k_ref[0, pl.ds(start, tq), :])
        if masked:
            s = _causal_mask(s, tq)
        m_new = jnp.maximum(m, jnp.max(s, axis=-1, keepdims=True))
        alpha = jnp.exp2(m - m_new)
        p = jnp.exp2(s - m_new)
        l = alpha * l + jnp.sum(p, axis=-1, keepdims=True)
        acc = alpha * acc + _dot(p.astype(BF16), v_ref[0, pl.ds(start, tq), :])
        return m_new, l, acc

    init = (jnp.full((2 * tq, 1), MASK_VALUE, F32), jnp.zeros((2 * tq, 1), F32),
            jnp.zeros((2 * tq, DIFF_V_DIM), F32))
    carry = lax.fori_loop(0, i, functools.partial(step, masked=False), init)
    _, l, acc = step(i, carry, masked=True)
    lam = _diff_lambda(lq1_ref, lk1_ref, lq2_ref, lk2_ref, lam_init)
    o_ref[0] = _diff_finalize(acc, l, lam, sg_ref[...], lam_init)


def _diff_attn_plain_kernel(lq1_ref, lk1_ref, lq2_ref, lk2_ref, sg_ref, q_ref, k_ref, v_ref,
                            o_ref, *, tq, lam_init, sub=MXU_EDGE):
    seq = q_ref.shape[1]
    lam = _diff_lambda(lq1_ref, lk1_ref, lq2_ref, lk2_ref, lam_init)

    def pv(s, start, size):
        v = v_ref[0, start:start + size, :]
        v_ones = jnp.concatenate([v, jnp.ones_like(v)], axis=1)
        return _dot(jnp.exp2(s).astype(BF16), v_ones)

    for i in reversed(range(seq // tq)):
        r0 = i * tq
        qs = _stack_maps(q_ref[0, r0:r0 + tq, :])
        acc = None
        for c in range(tq // sub):
            n = tq - c * sub
            qs_c = qs if c == 0 else jnp.concatenate([qs[c * sub:tq], qs[tq + c * sub:]], axis=0)
            s = _dot_nt(qs_c, k_ref[0, r0 + c * sub:r0 + (c + 1) * sub, :])
            pieces = [_causal_mask(s[:sub], sub), s[sub:n],
                      _causal_mask(s[n:n + sub], sub), s[n + sub:]]
            s = jnp.concatenate([p for p in pieces if p.shape[0]], axis=0)
            part = pv(s, r0 + c * sub, sub)
            if c == 0:
                acc = part
            else:
                acc = jnp.concatenate([acc[:c * sub], acc[c * sub:tq] + part[:n],
                                       acc[tq:tq + c * sub], acc[tq + c * sub:] + part[n:]],
                                      axis=0)
        for j in range(i):
            acc = acc + pv(_dot_nt(qs, k_ref[0, j * tq:(j + 1) * tq, :]), j * tq, tq)
        o_ref[0, r0:r0 + tq, :] = _diff_finalize(acc[:, :DIFF_V_DIM], acc[:, DIFF_V_DIM:], lam,
                                                 sg_ref[...], lam_init)


def _diff_attn(proj3d, lq1, lk1, lq2, lk2, subln_g, lam_init, tq, online):
    b, s, _ = proj3d.shape
    rows = tq if online else s
    grid = (b, DIFF_HEADS, s // rows)
    vec = pl.BlockSpec((1, DIFF_HEAD_DIM), lambda bi, h, i: (0, 0))
    body = _diff_attn_online_kernel if online else _diff_attn_plain_kernel
    return pl.pallas_call(
        functools.partial(body, tq=tq, lam_init=lam_init),
        grid=grid,
        in_specs=[
            vec, vec, vec, vec,
            pl.BlockSpec((1, DIFF_V_DIM), lambda bi, h, i: (0, 0)),
            pl.BlockSpec((1, rows, DIFF_V_DIM), lambda bi, h, i: (bi, i, COL_Q * DIFF_HEADS + h)),
            pl.BlockSpec((1, s, DIFF_V_DIM), lambda bi, h, i: (bi, 0, COL_K * DIFF_HEADS + h)),
            pl.BlockSpec((1, s, DIFF_V_DIM), lambda bi, h, i: (bi, 0, COL_V * DIFF_HEADS + h)),
        ],
        out_specs=pl.BlockSpec((1, rows, DIFF_V_DIM), lambda bi, h, i: (bi, i, h)),
        out_shape=jax.ShapeDtypeStruct((b, s, DIFF_HEADS * DIFF_V_DIM), BF16),
        compiler_params=pltpu.CompilerParams(
            dimension_semantics=("parallel", "parallel", "parallel"),
            vmem_limit_bytes=VMEM_LIMIT),
        name="diff_attn_online" if online else "diff_attn",
    )(lq1, lk1, lq2, lk2, subln_g, proj3d, proj3d, proj3d)


def _merge_kernel(x_ref, on_ref, yc_ref, qm_ref, ga_ref, gc_ref, gm_ref, kt_ref, vm_ref,
                  wa_ref, wc_ref, wm_ref, wo_ref, o_ref):
    ya = _dot(on_ref[...], wa_ref[...])
    yc = _dot(yc_ref[...], wc_ref[...])

    heads = []
    for h in range(MEM_HEADS):
        sl = slice(h * MEM_HEAD_DIM, (h + 1) * MEM_HEAD_DIM)
        s = _dot(qm_ref[:, sl], kt_ref[0, sl, :])
        p = jnp.exp(s - jnp.max(s, axis=-1, keepdims=True))
        p = p / jnp.sum(p, axis=-1, keepdims=True)
        heads.append(_dot(p.astype(BF16), vm_ref[0, :, sl]).astype(BF16))
    ym = _dot(jnp.concatenate(heads, axis=-1), wm_ref[...])

    merged = (ga_ref[...].astype(F32) * ya + gc_ref[...].astype(F32) * yc
              + gm_ref[...].astype(F32) * ym)
    o_ref[...] = x_ref[...] + _dot(merged.astype(BF16), wo_ref[...])


def _merge(x2d, on2d, proj, kt, vm, wa, wc, wm, wo, tm, seq):
    m, d = x2d.shape
    tiles_per_seq = seq // tm

    def col(c):
        return pl.BlockSpec((tm, d), lambda i: (i, c))

    mem_len = vm.shape[1]
    return pl.pallas_call(
        _merge_kernel,
        grid=(m // tm,),
        in_specs=[
            col(0), col(0),
            col(COL_YC), col(COL_QM), col(COL_GATE), col(COL_GATE + 1), col(COL_GATE + 2),
            pl.BlockSpec((1, d, mem_len), lambda i: (i // tiles_per_seq, 0, 0)),
            pl.BlockSpec((1, mem_len, d), lambda i: (i // tiles_per_seq, 0, 0)),
            _resident((d, d)), _resident((d, d)), _resident((d, d)), _resident((d, d)),
        ],
        out_specs=col(0),
        out_shape=jax.ShapeDtypeStruct((m, d), F32),
        compiler_params=pltpu.CompilerParams(
            dimension_semantics=("parallel",), vmem_limit_bytes=VMEM_LIMIT),
        name="merge",
    )(x2d, on2d, proj, proj, proj, proj, proj, kt, vm, wa, wc, wm, wo)


def _mlp_kernel(x_ref, g_ref, w1_ref, w2_ref, o_ref, *, tf):
    x = x_ref[...]
    h = _rms(x, g_ref[...]).astype(BF16)
    acc = x
    for f in range(w1_ref.shape[1] // tf):
        u = jnp.square(jnp.maximum(_dot(h, w1_ref[:, f * tf:(f + 1) * tf]), 0.0))
        acc = acc + _dot(u.astype(BF16), w2_ref[f * tf:(f + 1) * tf, :])
    o_ref[...] = acc


def _mlp(x2d, g, w1, w2, tm, tf):
    m, d = x2d.shape
    return pl.pallas_call(
        functools.partial(_mlp_kernel, tf=tf),
        grid=(m // tm,),
        in_specs=[
            pl.BlockSpec((tm, d), lambda i: (i, 0)),
            _resident((1, d)), _resident(w1.shape), _resident(w2.shape),
        ],
        out_specs=pl.BlockSpec((tm, d), lambda i: (i, 0)),
        out_shape=jax.ShapeDtypeStruct((m, d), F32),
        compiler_params=pltpu.CompilerParams(
            dimension_semantics=("parallel",), vmem_limit_bytes=VMEM_LIMIT),
        name="mlp",
    )(x2d, g, w1, w2)


def _layer(x, mem, lam_init, norm_mix_g, norm_mem_g, w_in, b_gate, q_norm_g, k_norm_g,
           lam_q1, lam_k1, lam_q2, lam_k2, subln_g, w_attn_o, conv_w, w_conv_o, w_mem_kv,
           mq_norm_g, mk_norm_g, w_mem_o, w_o, norm_mlp_g, w_mlp_in, w_mlp_out):
    b, s, d = x.shape
    x2d = x.reshape(b * s, d)
    row = lambda v: v.reshape(1, -1)

    gn = jnp.stack([
        jnp.tile(q_norm_g * (LOG2_E * DIFF_HEAD_DIM ** -0.5), d // DIFF_HEAD_DIM),
        jnp.tile(k_norm_g, d // DIFF_HEAD_DIM),
        jnp.tile(mq_norm_g * MEM_HEAD_DIM ** -0.5, d // MEM_HEAD_DIM),
    ]).reshape(3, 1, d)

    kt, vm = _mem_kv(mem, row(norm_mem_g), w_mem_kv.astype(BF16), row(mk_norm_g))
    proj = _in_proj(x2d, row(norm_mix_g), w_in.astype(BF16), gn, conv_w, row(b_gate),
                    tm=512, seq=s)

    score_bound = DIFF_HEAD_DIM ** 0.5 * jnp.max(jnp.abs(q_norm_g)) * jnp.max(jnp.abs(k_norm_g))
    attn = functools.partial(_diff_attn, proj.reshape(b, s, -1), row(lam_q1), row(lam_k1),
                             row(lam_q2), row(lam_k2), row(subln_g), lam_init)
    o_n = lax.cond(score_bound <= MAX_UNSHIFTED_LOGIT,
                   lambda: attn(tq=1024, online=False), lambda: attn(tq=512, online=True))
    x1 = _merge(x2d, o_n.reshape(b * s, d), proj, kt, vm,
                w_attn_o.astype(BF16), w_conv_o.astype(BF16), w_mem_o.astype(BF16),
                w_o.astype(BF16), tm=512, seq=s)
    x2 = _mlp(x1, row(norm_mlp_g), w_mlp_in.astype(BF16), w_mlp_out.astype(BF16), tm=1024, tf=1024)
    return x2.reshape(b, s, d)


def kernel(x, mem, norm_mix_g, norm_mem_g, w_in, b_gate, q_norm_g, k_norm_g, lam_q1, lam_k1,
           lam_q2, lam_k2, subln_g, w_attn_o, conv_w, w_conv_o, w_mem_kv, mq_norm_g, mk_norm_g,
           w_mem_o, w_o, norm_mlp_g, w_mlp_in, w_mlp_out):
    depth = w_in.shape[0]
    for layer in range(depth):
        lam_init = 0.8 - 0.6 * float(np.exp(-0.3 * layer))
        x = _layer(x, mem, lam_init, norm_mix_g[layer], norm_mem_g[layer], w_in[layer],
                   b_gate[layer], q_norm_g[layer], k_norm_g[layer], lam_q1[layer], lam_k1[layer],
                   lam_q2[layer], lam_k2[layer], subln_g[layer], w_attn_o[layer], conv_w[layer],
                   w_conv_o[layer], w_mem_kv[layer], mq_norm_g[layer], mk_norm_g[layer],
                   w_mem_o[layer], w_o[layer], norm_mlp_g[layer], w_mlp_in[layer],
                   w_mlp_out[layer])
    return x
```

```python
import functools

import jax
import jax.numpy as jnp
import numpy as np
from jax import lax
from jax.experimental import pallas as pl
from jax.experimental.pallas import tpu as pltpu

F32 = jnp.float32
BF16 = jnp.bfloat16

NORM_EPS = 1e-6
MASK_VALUE = -1e30
LOG2_E = 1.4426950408889634
MAX_UNSHIFTED_LOGIT = 60.0

DIFF_HEADS = 8
DIFF_HEAD_DIM = 64
DIFF_V_DIM = 2 * DIFF_HEAD_DIM
MEM_HEADS = 4
MEM_HEAD_DIM = 256
CONV_K = 3

MXU_EDGE = 256
HALO = 8

W_Q, W_K, W_V, W_XC, W_GB, W_GC, W_QM, W_GATE = 0, 1, 2, 3, 4, 5, 6, 7
COL_Q, COL_K, COL_V, COL_YC, COL_OM, COL_GATE = 0, 1, 2, 3, 4, 5
N_COL_BLOCKS = 8

VMEM_LIMIT = 56 * 1024 * 1024


def _rms(x, g):
    ms = jnp.mean(x * x, axis=-1, keepdims=True)
    return x * lax.rsqrt(ms + NORM_EPS) * g


def _dot(a, b):
    return jnp.dot(a, b, preferred_element_type=F32)


def _dot_nt(a, b):
    return lax.dot_general(a, b, (((1,), (1,)), ((), ())), preferred_element_type=F32)


def _mem_kv_kernel(mem_ref, g_ref, w_ref, kg_ref, kt_ref, v_ref):
    mem_n = _rms(mem_ref[0], g_ref[...]).astype(BF16)
    kv = _dot(mem_n, w_ref[...])
    width = kt_ref.shape[1]
    for h in range(MEM_HEADS):
        sl = slice(h * MEM_HEAD_DIM, (h + 1) * MEM_HEAD_DIM)
        kn = _rms(kv[:, sl], kg_ref[...])
        kt_ref[0, sl, :] = kn.T.astype(BF16)
    v_ref[0] = kv[:, width:].astype(BF16)


def _mem_kv(mem, g, w_kv, kg):
    b, m, d = mem.shape
    width = w_kv.shape[1] // 2
    return pl.pallas_call(
        _mem_kv_kernel,
        grid=(b,),
        in_specs=[
            pl.BlockSpec((1, m, d), lambda i: (i, 0, 0)),
            pl.BlockSpec((1, d), lambda i: (0, 0)),
            pl.BlockSpec((d, 2 * width), lambda i: (0, 0)),
            pl.BlockSpec((1, MEM_HEAD_DIM), lambda i: (0, 0)),
        ],
        out_specs=[
            pl.BlockSpec((1, width, m), lambda i: (i, 0, 0)),
            pl.BlockSpec((1, m, width), lambda i: (i, 0, 0)),
        ],
        out_shape=[
            jax.ShapeDtypeStruct((b, width, m), BF16),
            jax.ShapeDtypeStruct((b, m, width), BF16),
        ],
        compiler_params=pltpu.CompilerParams(
            dimension_semantics=("parallel",), vmem_limit_bytes=VMEM_LIMIT),
        name="mem_kv",
    )(mem, g, w_kv, kg)


def _in_proj_kernel(x_ref, g_ref, w_ref, gn_ref, cw_ref, bg_ref, kt_ref, vm_ref, o_ref,
                    inner_scr, *, tiles_per_seq):
    i = pl.program_id(0)
    tm, d = x_ref.shape

    @pl.when(i % tiles_per_seq == 0)
    def _():
        inner_scr[0:HALO, :] = jnp.zeros((HALO, d), F32)

    h = _rms(x_ref[...], g_ref[...]).astype(BF16)

    def proj(c):
        return _dot(h, w_ref[:, c * d:(c + 1) * d])

    def put(c, val):
        o_ref[:, c * d:(c + 1) * d] = val.astype(BF16)

    def gates():
        for t in range(3):
            z = proj(W_GATE + t) + bg_ref[:, t * d:(t + 1) * d]
            put(COL_GATE + t, 0.5 + 0.5 * jnp.tanh(0.5 * z))

    def conv():
        inner = proj(W_GC) * proj(W_XC)
        inner_scr[HALO:HALO + tm, :] = inner
        taps = (cw_ref[2:3, :] * inner
                + cw_ref[1:2, :] * inner_scr[HALO - 1:HALO - 1 + tm, :]
                + cw_ref[0:1, :] * inner_scr[HALO - 2:HALO - 2 + tm, :])
        put(COL_YC, proj(W_GB) * taps)
        inner_scr[0:HALO, :] = inner_scr[tm:tm + HALO, :]

    def group_normed(slot, src, dst):
        acc = proj(src)
        lane = lax.broadcasted_iota(jnp.int32, (tm, 2 * DIFF_HEAD_DIM), 1)
        low = lane < DIFF_HEAD_DIM
        for c in range(d // (2 * DIFF_HEAD_DIM)):
            sl = slice(c * 2 * DIFF_HEAD_DIM, (c + 1) * 2 * DIFF_HEAD_DIM)
            a = acc[:, sl]
            sq = a * a
            s_lo = jnp.sum(jnp.where(low, sq, 0.0), axis=-1, keepdims=True)
            s_hi = jnp.sum(jnp.where(low, 0.0, sq), axis=-1, keepdims=True)
            ms = jnp.where(low, s_lo, s_hi) * (1.0 / DIFF_HEAD_DIM)
            y = a * lax.rsqrt(ms + NORM_EPS) * gn_ref[slot, :, sl]
            o_ref[:, dst * d + sl.start:dst * d + sl.stop] = y.astype(BF16)

    def memory_attention():
        acc = proj(W_QM)
        for hd in range(MEM_HEADS):
            sl = slice(hd * MEM_HEAD_DIM, (hd + 1) * MEM_HEAD_DIM)
            qn = _rms(acc[:, sl], gn_ref[2, :, sl]).astype(BF16)
            s = _dot(qn, kt_ref[0, sl, :])
            p = jnp.exp(s - jnp.max(s, axis=-1, keepdims=True))
            p = p / jnp.sum(p, axis=-1, keepdims=True)
            o_ref[:, COL_OM * d + sl.start:COL_OM * d + sl.stop] = _dot(
                p.astype(BF16), vm_ref[0, :, sl]).astype(BF16)

    memory_attention()
    group_normed(1, W_K, COL_K)
    group_normed(0, W_Q, COL_Q)
    gates()
    conv()
    put(COL_V, proj(W_V))


def _resident(shape):
    return pl.BlockSpec(shape, lambda i: (0,) * len(shape), pipeline_mode=pl.Buffered(1))


def _in_proj(x2d, g, w_in, gn, conv_w, b_gate, kt, vm, tm, seq):
    m, d = x2d.shape
    tiles_per_seq = seq // tm
    return pl.pallas_call(
        functools.partial(_in_proj_kernel, tiles_per_seq=tiles_per_seq),
        grid=(m // tm,),
        in_specs=[
            pl.BlockSpec((tm, d), lambda i: (i, 0)),
            _resident((1, d)), _resident(w_in.shape),
            _resident(gn.shape), _resident(conv_w.shape), _resident(b_gate.shape),
            pl.BlockSpec((1,) + kt.shape[1:], lambda i: (i // tiles_per_seq, 0, 0)),
            pl.BlockSpec((1,) + vm.shape[1:], lambda i: (i // tiles_per_seq, 0, 0)),
        ],
        out_specs=pl.BlockSpec((tm, N_COL_BLOCKS * d), lambda i: (i, 0)),
        out_shape=jax.ShapeDtypeStruct((m, N_COL_BLOCKS * d), BF16),
        scratch_shapes=[pltpu.VMEM((HALO + tm, d), F32)],
        compiler_params=pltpu.CompilerParams(
            dimension_semantics=("arbitrary",), vmem_limit_bytes=VMEM_LIMIT),
        name="in_proj",
    )(x2d, g, w_in, gn, conv_w, b_gate, kt, vm)


def _stack_maps(q):
    lane = lax.broadcasted_iota(jnp.int32, q.shape, 1)
    zero = jnp.zeros_like(q)
    return jnp.concatenate([jnp.where(lane < DIFF_HEAD_DIM, q, zero),
                            jnp.where(lane >= DIFF_HEAD_DIM, q, zero)], axis=0)


def _causal_mask(s, n):
    row = lax.broadcasted_iota(jnp.int32, (n, n), 0)
    col = lax.broadcasted_iota(jnp.int32, (n, n), 1)
    keep = jnp.concatenate([col <= row] * (s.shape[0] // n), axis=0)
    return jnp.where(keep, s, MASK_VALUE)


def _diff_lambda(lq1_ref, lk1_ref, lq2_ref, lk2_ref, lam_init):
    return (jnp.exp(jnp.sum(lq1_ref[...] * lk1_ref[...], axis=-1, keepdims=True))
            - jnp.exp(jnp.sum(lq2_ref[...] * lk2_ref[...], axis=-1, keepdims=True))
            + lam_init)


def _diff_finalize(acc, l, lam, sg, lam_init):
    n = acc.shape[0] // 2
    o = acc[:n] / l[:n] - lam * (acc[n:] / l[n:])
    return (_rms(o, sg) * (1.0 - lam_init)).astype(BF16)


def _diff_attn_online_kernel(lq1_ref, lk1_ref, lq2_ref, lk2_ref, sg_ref, q_ref, k_ref, v_ref,
                             o_ref, *, tq, lam_init):
    i = pl.program_id(2)
    qs = _stack_maps(q_ref[0])

    def step(j, carry, masked):
        m, l, acc = carry
        start = pl.multiple_of(j * tq, tq)
        s = _dot_nt(qs, k_ref[0, pl.ds(start, tq), :])
        if masked:
            s = _causal_mask(s, tq)
        m_new = jnp.maximum(m, jnp.max(s, axis=-1, keepdims=True))
        alpha = jnp.exp2(m - m_new)
        p = jnp.exp2(s - m_new)
        l = alpha * l + jnp.sum(p, axis=-1, keepdims=True)
        acc = alpha * acc + _dot(p.astype(BF16), v_ref[0, pl.ds(start, tq), :])
        return m_new, l, acc

    init = (jnp.full((2 * tq, 1), MASK_VALUE, F32), jnp.zeros((2 * tq, 1), F32),
            jnp.zeros((2 * tq, DIFF_V_DIM), F32))
    carry = lax.fori_loop(0, i, functools.partial(step, masked=False), init)
    _, l, acc = step(i, carry, masked=True)
    lam = _diff_lambda(lq1_ref, lk1_ref, lq2_ref, lk2_ref, lam_init)
    o_ref[0] = _diff_finalize(acc, l, lam, sg_ref[...], lam_init)


def _diff_attn_plain_kernel(lq1_ref, lk1_ref, lq2_ref, lk2_ref, sg_ref, q_ref, k_ref, v_ref,
                            o_ref, *, tq, lam_init, sub=MXU_EDGE):
    seq = q_ref.shape[1]
    lam = _diff_lambda(lq1_ref, lk1_ref, lq2_ref, lk2_ref, lam_init)

    def pv(s, start, size):
        v = v_ref[0, start:start + size, :]
        v_ones = jnp.concatenate([v, jnp.ones_like(v)], axis=1)
        return _dot(jnp.exp2(s).astype(BF16), v_ones)

    for i in reversed(range(seq // tq)):
        r0 = i * tq
        qs = _stack_maps(q_ref[0, r0:r0 + tq, :])
        acc = None
        for c in range(tq // sub):
            n = tq - c * sub
            qs_c = qs if c == 0 else jnp.concatenate([qs[c * sub:tq], qs[tq + c * sub:]], axis=0)
            s = _dot_nt(qs_c, k_ref[0, r0 + c * sub:r0 + (c + 1) * sub, :])
            pieces = [_causal_mask(s[:sub], sub), s[sub:n],
                      _causal_mask(s[n:n + sub], sub), s[n + sub:]]
            s = jnp.concatenate([p for p in pieces if p.shape[0]], axis=0)
            part = pv(s, r0 + c * sub, sub)
            if c == 0:
                acc = part
            else:
                acc = jnp.concatenate([acc[:c * sub], acc[c * sub:tq] + part[:n],
                                       acc[tq:tq + c * sub], acc[tq + c * sub:] + part[n:]],
                                      axis=0)
        for j in range(i):
            acc = acc + pv(_dot_nt(qs, k_ref[0, j * tq:(j + 1) * tq, :]), j * tq, tq)
        o_ref[0, r0:r0 + tq, :] = _diff_finalize(acc[:, :DIFF_V_DIM], acc[:, DIFF_V_DIM:], lam,
                                                 sg_ref[...], lam_init)


def _diff_attn(proj3d, lq1, lk1, lq2, lk2, subln_g, lam_init, tq, online):
    b, s, _ = proj3d.shape
    rows = tq if online else s
    grid = (b, DIFF_HEADS, s // rows)
    vec = pl.BlockSpec((1, DIFF_HEAD_DIM), lambda bi, h, i: (0, 0))
    body = _diff_attn_online_kernel if online else _diff_attn_plain_kernel
    return pl.pallas_call(
        functools.partial(body, tq=tq, lam_init=lam_init),
        grid=grid,
        in_specs=[
            vec, vec, vec, vec,
            pl.BlockSpec((1, DIFF_V_DIM), lambda bi, h, i: (0, 0)),
            pl.BlockSpec((1, rows, DIFF_V_DIM), lambda bi, h, i: (bi, i, COL_Q * DIFF_HEADS + h)),
            pl.BlockSpec((1, s, DIFF_V_DIM), lambda bi, h, i: (bi, 0, COL_K * DIFF_HEADS + h)),
            pl.BlockSpec((1, s, DIFF_V_DIM), lambda bi, h, i: (bi, 0, COL_V * DIFF_HEADS + h)),
        ],
        out_specs=pl.BlockSpec((1, rows, DIFF_V_DIM), lambda bi, h, i: (bi, i, h)),
        out_shape=jax.ShapeDtypeStruct((b, s, DIFF_HEADS * DIFF_V_DIM), BF16),
        compiler_params=pltpu.CompilerParams(
            dimension_semantics=("parallel", "parallel", "parallel"),
            vmem_limit_bytes=VMEM_LIMIT),
        name="diff_attn_online" if online else "diff_attn",
    )(lq1, lk1, lq2, lk2, subln_g, proj3d, proj3d, proj3d)


def _merge_kernel(x_ref, on_ref, yc_ref, om_ref, ga_ref, gc_ref, gm_ref,
                  wa_ref, wc_ref, wm_ref, wo_ref, o_ref):
    ya = _dot(on_ref[...], wa_ref[...])
    yc = _dot(yc_ref[...], wc_ref[...])
    ym = _dot(om_ref[...], wm_ref[...])

    merged = (ga_ref[...].astype(F32) * ya + gc_ref[...].astype(F32) * yc
              + gm_ref[...].astype(F32) * ym)
    o_ref[...] = x_ref[...] + _dot(merged.astype(BF16), wo_ref[...])


def _merge(x2d, on2d, proj, wa, wc, wm, wo, tm):
    m, d = x2d.shape

    def col(c):
        return pl.BlockSpec((tm, d), lambda i: (i, c))

    return pl.pallas_call(
        _merge_kernel,
        grid=(m // tm,),
        in_specs=[
            col(0), col(0),
            col(COL_YC), col(COL_OM), col(COL_GATE), col(COL_GATE + 1), col(COL_GATE + 2),
            _resident((d, d)), _resident((d, d)), _resident((d, d)), _resident((d, d)),
        ],
        out_specs=col(0),
        out_shape=jax.ShapeDtypeStruct((m, d), F32),
        compiler_params=pltpu.CompilerParams(
            dimension_semantics=("parallel",), vmem_limit_bytes=VMEM_LIMIT),
        name="merge",
    )(x2d, on2d, proj, proj, proj, proj, proj, wa, wc, wm, wo)


def _mlp_kernel(x_ref, g_ref, w1_ref, w2_ref, o_ref, *, tf):
    x = x_ref[...]
    h = _rms(x, g_ref[...]).astype(BF16)
    acc = x
    for f in range(w1_ref.shape[1] // tf):
        u = jnp.square(jnp.maximum(_dot(h, w1_ref[:, f * tf:(f + 1) * tf]), 0.0))
        acc = acc + _dot(u.astype(BF16), w2_ref[f * tf:(f + 1) * tf, :])
    o_ref[...] = acc


def _mlp(x2d, g, w1, w2, tm, tf):
    m, d = x2d.shape
    return pl.pallas_call(
        functools.partial(_mlp_kernel, tf=tf),
        grid=(m // tm,),
        in_specs=[
            pl.BlockSpec((tm, d), lambda i: (i, 0)),
            _resident((1, d)), _resident(w1.shape), _resident(w2.shape),
        ],
        out_specs=pl.BlockSpec((tm, d), lambda i: (i, 0)),
        out_shape=jax.ShapeDtypeStruct((m, d), F32),
        compiler_params=pltpu.CompilerParams(
            dimension_semantics=("parallel",), vmem_limit_bytes=VMEM_LIMIT),
        name="mlp",
    )(x2d, g, w1, w2)


def _layer(x, mem, lam_init, norm_mix_g, norm_mem_g, w_in, b_gate, q_norm_g, k_norm_g,
           lam_q1, lam_k1, lam_q2, lam_k2, subln_g, w_attn_o, conv_w, w_conv_o, w_mem_kv,
           mq_norm_g, mk_norm_g, w_mem_o, w_o, norm_mlp_g, w_mlp_in, w_mlp_out):
    b, s, d = x.shape
    x2d = x.reshape(b * s, d)
    row = lambda v: v.reshape(1, -1)

    gn = jnp.stack([
        jnp.tile(q_norm_g * (LOG2_E * DIFF_HEAD_DIM ** -0.5), d // DIFF_HEAD_DIM),
        jnp.tile(k_norm_g, d // DIFF_HEAD_DIM),
        jnp.tile(mq_norm_g * MEM_HEAD_DIM ** -0.5, d // MEM_HEAD_DIM),
    ]).reshape(3, 1, d)

    kt, vm = _mem_kv(mem, row(norm_mem_g), w_mem_kv.astype(BF16), row(mk_norm_g))
    proj = _in_proj(x2d, row(norm_mix_g), w_in.astype(BF16), gn, conv_w, row(b_gate),
                    kt, vm, tm=512, seq=s)

    score_bound = DIFF_HEAD_DIM ** 0.5 * jnp.max(jnp.abs(q_norm_g)) * jnp.max(jnp.abs(k_norm_g))
    attn = functools.partial(_diff_attn, proj.reshape(b, s, -1), row(lam_q1), row(lam_k1),
                             row(lam_q2), row(lam_k2), row(subln_g), lam_init)
    o_n = lax.cond(score_bound <= MAX_UNSHIFTED_LOGIT,
                   lambda: attn(tq=1024, online=False), lambda: attn(tq=512, online=True))
    x1 = _merge(x2d, o_n.reshape(b * s, d), proj,
                w_attn_o.astype(BF16), w_conv_o.astype(BF16), w_mem_o.astype(BF16),
                w_o.astype(BF16), tm=512)
    x2 = _mlp(x1, row(norm_mlp_g), w_mlp_in.astype(BF16), w_mlp_out.astype(BF16), tm=1024, tf=1024)
    return x2.reshape(b, s, d)


def kernel(x, mem, norm_mix_g, norm_mem_g, w_in, b_gate, q_norm_g, k_norm_g, lam_q1, lam_k1,
           lam_q2, lam_k2, subln_g, w_attn_o, conv_w, w_conv_o, w_mem_kv, mq_norm_g, mk_norm_g,
           w_mem_o, w_o, norm_mlp_g, w_mlp_in, w_mlp_out):
    depth = w_in.shape[0]
    for layer in range(depth):
        lam_init = 0.8 - 0.6 * float(np.exp(-0.3 * layer))
        x = _layer(x, mem, lam_init, norm_mix_g[layer], norm_mem_g[layer], w_in[layer],
                   b_gate[layer], q_norm_g[layer], k_norm_g[layer], lam_q1[layer], lam_k1[layer],
                   lam_q2[layer], lam_k2[layer], subln_g[layer], w_attn_o[layer], conv_w[layer],
                   w_conv_o[layer], w_mem_kv[layer], mq_norm_g[layer], mk_norm_g[layer],
                   w_mem_o[layer], w_o[layer], norm_mlp_g[layer], w_mlp_in[layer],
                   w_mlp_out[layer])
    return x
```

```python
import functools

import jax
import jax.numpy as jnp
import numpy as np
from jax import lax
from jax.experimental import pallas as pl
from jax.experimental.pallas import tpu as pltpu

F32 = jnp.float32
BF16 = jnp.bfloat16

NORM_EPS = 1e-6
MASK_VALUE = -1e30
LOG2_E = 1.4426950408889634
MAX_UNSHIFTED_LOGIT = 60.0

DIFF_HEADS = 8
DIFF_HEAD_DIM = 64
DIFF_V_DIM = 2 * DIFF_HEAD_DIM
MEM_HEADS = 4
MEM_HEAD_DIM = 256
CONV_K = 3

MXU_EDGE = 256
HALO = 8

W_Q, W_K, W_V, W_XC, W_GB, W_GC, W_QM, W_GATE = 0, 1, 2, 3, 4, 5, 6, 7
COL_Q, COL_K, COL_V, COL_YC, COL_OM, COL_GATE = 0, 1, 2, 3, 4, 5
N_COL_BLOCKS = 8

VMEM_LIMIT = 56 * 1024 * 1024


def _rms(x, g):
    ms = jnp.mean(x * x, axis=-1, keepdims=True)
    return x * lax.rsqrt(ms + NORM_EPS) * g


def _dot(a, b):
    return jnp.dot(a, b, preferred_element_type=F32)


def _dot_nt(a, b):
    return lax.dot_general(a, b, (((1,), (1,)), ((), ())), preferred_element_type=F32)


def _mem_kv_kernel(mem_ref, g_ref, w_ref, kg_ref, kt_ref, v_ref):
    mem_n = _rms(mem_ref[0], g_ref[...]).astype(BF16)
    kv = _dot(mem_n, w_ref[...])
    width = kt_ref.shape[1]
    for h in range(MEM_HEADS):
        sl = slice(h * MEM_HEAD_DIM, (h + 1) * MEM_HEAD_DIM)
        kn = _rms(kv[:, sl], kg_ref[...])
        kt_ref[0, sl, :] = kn.T.astype(BF16)
    v_ref[0] = kv[:, width:].astype(BF16)


def _mem_kv(mem, g, w_kv, kg):
    b, m, d = mem.shape
    width = w_kv.shape[1] // 2
    return pl.pallas_call(
        _mem_kv_kernel,
        grid=(b,),
        in_specs=[
            pl.BlockSpec((1, m, d), lambda i: (i, 0, 0)),
            pl.BlockSpec((1, d), lambda i: (0, 0)),
            pl.BlockSpec((d, 2 * width), lambda i: (0, 0)),
            pl.BlockSpec((1, MEM_HEAD_DIM), lambda i: (0, 0)),
        ],
        out_specs=[
            pl.BlockSpec((1, width, m), lambda i: (i, 0, 0)),
            pl.BlockSpec((1, m, width), lambda i: (i, 0, 0)),
        ],
        out_shape=[
            jax.ShapeDtypeStruct((b, width, m), BF16),
            jax.ShapeDtypeStruct((b, m, width), BF16),
        ],
        compiler_params=pltpu.CompilerParams(
            dimension_semantics=("parallel",), vmem_limit_bytes=VMEM_LIMIT),
        name="mem_kv",
    )(mem, g, w_kv, kg)


def _in_proj_kernel(x_ref, g_ref, w_ref, gn_ref, cw_ref, bg_ref, kt_ref, vm_ref, o_ref,
                    inner_scr, *, tiles_per_seq):
    i = pl.program_id(0)
    tm, d = x_ref.shape

    @pl.when(i % tiles_per_seq == 0)
    def _():
        inner_scr[0:HALO, :] = jnp.zeros((HALO, d), F32)

    h = _rms(x_ref[...], g_ref[...]).astype(BF16)

    def proj(c):
        return _dot(h, w_ref[:, c * d:(c + 1) * d])

    def put(c, val):
        o_ref[:, c * d:(c + 1) * d] = val.astype(BF16)

    def gates():
        for t in range(3):
            z = proj(W_GATE + t) + bg_ref[:, t * d:(t + 1) * d]
            put(COL_GATE + t, 0.5 + 0.5 * jnp.tanh(0.5 * z))

    def conv():
        inner = proj(W_GC) * proj(W_XC)
        inner_scr[HALO:HALO + tm, :] = inner
        taps = (cw_ref[2:3, :] * inner
                + cw_ref[1:2, :] * inner_scr[HALO - 1:HALO - 1 + tm, :]
                + cw_ref[0:1, :] * inner_scr[HALO - 2:HALO - 2 + tm, :])
        put(COL_YC, proj(W_GB) * taps)
        inner_scr[0:HALO, :] = inner_scr[tm:tm + HALO, :]

    def group_normed(slot, src, dst):
        acc = proj(src)
        lane = lax.broadcasted_iota(jnp.int32, (tm, 2 * DIFF_HEAD_DIM), 1)
        low = lane < DIFF_HEAD_DIM
        for c in range(d // (2 * DIFF_HEAD_DIM)):
            sl = slice(c * 2 * DIFF_HEAD_DIM, (c + 1) * 2 * DIFF_HEAD_DIM)
            a = acc[:, sl]
            sq = a * a
            s_lo = jnp.sum(jnp.where(low, sq, 0.0), axis=-1, keepdims=True)
            s_hi = jnp.sum(jnp.where(low, 0.0, sq), axis=-1, keepdims=True)
            ms = jnp.where(low, s_lo, s_hi) * (1.0 / DIFF_HEAD_DIM)
            y = a * lax.rsqrt(ms + NORM_EPS) * gn_ref[slot, :, sl]
            o_ref[:, dst * d + sl.start:dst * d + sl.stop] = y.astype(BF16)

    def memory_attention():
        acc = proj(W_QM)
        for hd in range(MEM_HEADS):
            sl = slice(hd * MEM_HEAD_DIM, (hd + 1) * MEM_HEAD_DIM)
            qn = _rms(acc[:, sl], gn_ref[2, :, sl]).astype(BF16)
            s = _dot(qn, kt_ref[0, sl, :])
            p = jnp.exp(s - jnp.max(s, axis=-1, keepdims=True))
            p = p / jnp.sum(p, axis=-1, keepdims=True)
            o_ref[:, COL_OM * d + sl.start:COL_OM * d + sl.stop] = _dot(
                p.astype(BF16), vm_ref[0, :, sl]).astype(BF16)

    memory_attention()
    group_normed(1, W_K, COL_K)
    group_normed(0, W_Q, COL_Q)
    gates()
    conv()
    put(COL_V, proj(W_V))


def _resident(shape):
    return pl.BlockSpec(shape, lambda i: (0,) * len(shape), pipeline_mode=pl.Buffered(1))


def _in_proj(x2d, g, w_in, gn, conv_w, b_gate, kt, vm, tm, seq):
    m, d = x2d.shape
    tiles_per_seq = seq // tm
    return pl.pallas_call(
        functools.partial(_in_proj_kernel, tiles_per_seq=tiles_per_seq),
        grid=(m // tm,),
        in_specs=[
            pl.BlockSpec((tm, d), lambda i: (i, 0)),
            _resident((1, d)), _resident(w_in.shape),
            _resident(gn.shape), _resident(conv_w.shape), _resident(b_gate.shape),
            pl.BlockSpec((1,) + kt.shape[1:], lambda i: (i // tiles_per_seq, 0, 0)),
            pl.BlockSpec((1,) + vm.shape[1:], lambda i: (i // tiles_per_seq, 0, 0)),
        ],
        out_specs=pl.BlockSpec((tm, N_COL_BLOCKS * d), lambda i: (i, 0)),
        out_shape=jax.ShapeDtypeStruct((m, N_COL_BLOCKS * d), BF16),
        scratch_shapes=[pltpu.VMEM((HALO + tm, d), F32)],
        compiler_params=pltpu.CompilerParams(
            dimension_semantics=("arbitrary",), vmem_limit_bytes=VMEM_LIMIT),
        name="in_proj",
    )(x2d, g, w_in, gn, conv_w, b_gate, kt, vm)


def _stack_maps(q):
    lane = lax.broadcasted_iota(jnp.int32, q.shape, 1)
    zero = jnp.zeros_like(q)
    return jnp.concatenate([jnp.where(lane < DIFF_HEAD_DIM, q, zero),
                            jnp.where(lane >= DIFF_HEAD_DIM, q, zero)], axis=0)


def _causal_mask(s, n):
    row = lax.broadcasted_iota(jnp.int32, (n, n), 0)
    col = lax.broadcasted_iota(jnp.int32, (n, n), 1)
    keep = jnp.concatenate([col <= row] * (s.shape[0] // n), axis=0)
    return jnp.where(keep, s, MASK_VALUE)


def _diff_lambda(lq1_ref, lk1_ref, lq2_ref, lk2_ref, lam_init):
    return (jnp.exp(jnp.sum(lq1_ref[...] * lk1_ref[...], axis=-1, keepdims=True))
            - jnp.exp(jnp.sum(lq2_ref[...] * lk2_ref[...], axis=-1, keepdims=True))
            + lam_init)


def _diff_finalize(acc, l, lam, sg, lam_init):
    n = acc.shape[0] // 2
    o = acc[:n] / l[:n] - lam * (acc[n:] / l[n:])
    return (_rms(o, sg) * (1.0 - lam_init)).astype(BF16)


def _diff_attn_online_kernel(lq1_ref, lk1_ref, lq2_ref, lk2_ref, sg_ref, q_ref, k_ref, v_ref,
                             o_ref, *, tq, lam_init):
    i = pl.program_id(2)
    qs = _stack_maps(q_ref[0])

    def step(j, carry, masked):
        m, l, acc = carry
        start = pl.multiple_of(j * tq, tq)
        s = _dot_nt(qs, k_ref[0, pl.ds(start, tq), :])
        if masked:
            s = _causal_mask(s, tq)
        m_new = jnp.maximum(m, jnp.max(s, axis=-1, keepdims=True))
        alpha = jnp.exp2(m - m_new)
        p = jnp.exp2(s - m_new)
        l = alpha * l + jnp.sum(p, axis=-1, keepdims=True)
        acc = alpha * acc + _dot(p.astype(BF16), v_ref[0, pl.ds(start, tq), :])
        return m_new, l, acc

    init = (jnp.full((2 * tq, 1), MASK_VALUE, F32), jnp.zeros((2 * tq, 1), F32),
            jnp.zeros((2 * tq, DIFF_V_DIM), F32))
    carry = lax.fori_loop(0, i, functools.partial(step, masked=False), init)
    _, l, acc = step(i, carry, masked=True)
    lam = _diff_lambda(lq1_ref, lk1_ref, lq2_ref, lk2_ref, lam_init)
    o_ref[0] = _diff_finalize(acc, l, lam, sg_ref[...], lam_init)


def _diff_attn_plain_kernel(lq1_ref, lk1_ref, lq2_ref, lk2_ref, sg_ref, q_ref, k_ref, v_ref,
                            *refs, tq, lam_init, sub=MXU_EDGE):
    n_cast = len(refs) // 2
    o_ref = refs[n_cast]
    seq = q_ref.shape[1]
    lam = _diff_lambda(lq1_ref, lk1_ref, lq2_ref, lk2_ref, lam_init)

    def pv(s, start, size):
        v = v_ref[0, start:start + size, :]
        v_ones = jnp.concatenate([v, jnp.ones_like(v)], axis=1)
        return _dot(jnp.exp2(s).astype(BF16), v_ones)

    for i in reversed(range(seq // tq)):
        r0 = i * tq
        qs = _stack_maps(q_ref[0, r0:r0 + tq, :])
        acc = None
        for c in range(tq // sub):
            n = tq - c * sub
            qs_c = qs if c == 0 else jnp.concatenate([qs[c * sub:tq], qs[tq + c * sub:]], axis=0)
            s = _dot_nt(qs_c, k_ref[0, r0 + c * sub:r0 + (c + 1) * sub, :])
            pieces = [_causal_mask(s[:sub], sub), s[sub:n],
                      _causal_mask(s[n:n + sub], sub), s[n + sub:]]
            s = jnp.concatenate([p for p in pieces if p.shape[0]], axis=0)
            part = pv(s, r0 + c * sub, sub)
            if c == 0:
                acc = part
            else:
                acc = jnp.concatenate([acc[:c * sub], acc[c * sub:tq] + part[:n],
                                       acc[tq:tq + c * sub], acc[tq + c * sub:] + part[n:]],
                                      axis=0)
        for j in range(i):
            acc = acc + pv(_dot_nt(qs, k_ref[0, j * tq:(j + 1) * tq, :]), j * tq, tq)
        o_ref[0, r0:r0 + tq, :] = _diff_finalize(acc[:, :DIFF_V_DIM], acc[:, DIFF_V_DIM:], lam,
                                                 sg_ref[...], lam_init)
        if i == seq // tq - 1:
            for w_ref, wb_ref in zip(refs[:n_cast], refs[n_cast + 1:]):
                wb_ref[...] = w_ref[...].astype(BF16)


def _diff_attn(proj3d, lq1, lk1, lq2, lk2, subln_g, later_weights, lam_init, tq, online):
    b, s, _ = proj3d.shape
    rows = tq if online else s
    grid = (b, DIFF_HEADS, s // rows)
    vec = pl.BlockSpec((1, DIFF_HEAD_DIM), lambda bi, h, i: (0, 0))
    in_specs = [
        vec, vec, vec, vec,
        pl.BlockSpec((1, DIFF_V_DIM), lambda bi, h, i: (0, 0)),
        pl.BlockSpec((1, rows, DIFF_V_DIM), lambda bi, h, i: (bi, i, COL_Q * DIFF_HEADS + h)),
        pl.BlockSpec((1, s, DIFF_V_DIM), lambda bi, h, i: (bi, 0, COL_K * DIFF_HEADS + h)),
        pl.BlockSpec((1, s, DIFF_V_DIM), lambda bi, h, i: (bi, 0, COL_V * DIFF_HEADS + h)),
    ]
    out_specs = [pl.BlockSpec((1, rows, DIFF_V_DIM), lambda bi, h, i: (bi, i, h))]
    out_shape = [jax.ShapeDtypeStruct((b, s, DIFF_HEADS * DIFF_V_DIM), BF16)]
    if online:
        body, cast_in = _diff_attn_online_kernel, []
    else:
        body, cast_in = _diff_attn_plain_kernel, list(later_weights)
        for w in cast_in:
            chunk = pl.BlockSpec((w.shape[0] // (b * DIFF_HEADS), w.shape[1]),
                                 lambda bi, h, i: (bi * DIFF_HEADS + h, 0))
            in_specs.append(chunk)
            out_specs.append(chunk)
            out_shape.append(jax.ShapeDtypeStruct(w.shape, BF16))
    outs = pl.pallas_call(
        functools.partial(body, tq=tq, lam_init=lam_init),
        grid=grid,
        in_specs=in_specs,
        out_specs=out_specs,
        out_shape=out_shape,
        compiler_params=pltpu.CompilerParams(
            dimension_semantics=("parallel", "parallel", "parallel"),
            vmem_limit_bytes=VMEM_LIMIT),
        name="diff_attn_online" if online else "diff_attn",
    )(lq1, lk1, lq2, lk2, subln_g, proj3d, proj3d, proj3d, *cast_in)
    if online:
        return outs[0], tuple(w.astype(BF16) for w in later_weights)
    return outs[0], tuple(outs[1:])


def _merge_kernel(x_ref, on_ref, yc_ref, om_ref, ga_ref, gc_ref, gm_ref,
                  wa_ref, wc_ref, wm_ref, wo_ref, o_ref):
    ya = _dot(on_ref[...], wa_ref[...])
    yc = _dot(yc_ref[...], wc_ref[...])
    ym = _dot(om_ref[...], wm_ref[...])

    merged = (ga_ref[...].astype(F32) * ya + gc_ref[...].astype(F32) * yc
              + gm_ref[...].astype(F32) * ym)
    o_ref[...] = x_ref[...] + _dot(merged.astype(BF16), wo_ref[...])


def _merge(x2d, on2d, proj, wa, wc, wm, wo, tm):
    m, d = x2d.shape

    def col(c):
        return pl.BlockSpec((tm, d), lambda i: (i, c))

    return pl.pallas_call(
        _merge_kernel,
        grid=(m // tm,),
        in_specs=[
            col(0), col(0),
            col(COL_YC), col(COL_OM), col(COL_GATE), col(COL_GATE + 1), col(COL_GATE + 2),
            _resident((d, d)), _resident((d, d)), _resident((d, d)), _resident((d, d)),
        ],
        out_specs=col(0),
        out_shape=jax.ShapeDtypeStruct((m, d), F32),
        compiler_params=pltpu.CompilerParams(
            dimension_semantics=("parallel",), vmem_limit_bytes=VMEM_LIMIT),
        name="merge",
    )(x2d, on2d, proj, proj, proj, proj, proj, wa, wc, wm, wo)


def _mlp_kernel(x_ref, g_ref, w1_ref, w2_ref, o_ref, *, tf):
    x = x_ref[...]
    h = _rms(x, g_ref[...]).astype(BF16)
    acc = x
    for f in range(w1_ref.shape[1] // tf):
        u = jnp.square(jnp.maximum(_dot(h, w1_ref[:, f * tf:(f + 1) * tf]), 0.0))
        acc = acc + _dot(u.astype(BF16), w2_ref[f * tf:(f + 1) * tf, :])
    o_ref[...] = acc


def _mlp(x2d, g, w1, w2, tm, tf):
    m, d = x2d.shape
    return pl.pallas_call(
        functools.partial(_mlp_kernel, tf=tf),
        grid=(m // tm,),
        in_specs=[
            pl.BlockSpec((tm, d), lambda i: (i, 0)),
            _resident((1, d)), _resident(w1.shape), _resident(w2.shape),
        ],
        out_specs=pl.BlockSpec((tm, d), lambda i: (i, 0)),
        out_shape=jax.ShapeDtypeStruct((m, d), F32),
        compiler_params=pltpu.CompilerParams(
            dimension_semantics=("parallel",), vmem_limit_bytes=VMEM_LIMIT),
        name="mlp",
    )(x2d, g, w1, w2)


def _layer(x, mem, lam_init, norm_mix_g, norm_mem_g, w_in, b_gate, q_norm_g, k_norm_g,
           lam_q1, lam_k1, lam_q2, lam_k2, subln_g, w_attn_o, conv_w, w_conv_o, w_mem_kv,
           mq_norm_g, mk_norm_g, w_mem_o, w_o, norm_mlp_g, w_mlp_in, w_mlp_out):
    b, s, d = x.shape
    x2d = x.reshape(b * s, d)
    row = lambda v: v.reshape(1, -1)

    gn = jnp.stack([
        jnp.tile(q_norm_g * (LOG2_E * DIFF_HEAD_DIM ** -0.5), d // DIFF_HEAD_DIM),
        jnp.tile(k_norm_g, d // DIFF_HEAD_DIM),
        jnp.tile(mq_norm_g * MEM_HEAD_DIM ** -0.5, d // MEM_HEAD_DIM),
    ]).reshape(3, 1, d)

    kt, vm = _mem_kv(mem, row(norm_mem_g), w_mem_kv.astype(BF16), row(mk_norm_g))
    proj = _in_proj(x2d, row(norm_mix_g), w_in.astype(BF16), gn, conv_w, row(b_gate),
                    kt, vm, tm=512, seq=s)

    score_bound = DIFF_HEAD_DIM ** 0.5 * jnp.max(jnp.abs(q_norm_g)) * jnp.max(jnp.abs(k_norm_g))
    later = (w_attn_o, w_conv_o, w_mem_o, w_o, w_mlp_in, w_mlp_out)
    attn = functools.partial(_diff_attn, proj.reshape(b, s, -1), row(lam_q1), row(lam_k1),
                             row(lam_q2), row(lam_k2), row(subln_g), later, lam_init)
    o_n, (wa, wc, wm, wo, w1, w2) = lax.cond(
        score_bound <= MAX_UNSHIFTED_LOGIT,
        lambda: attn(tq=1024, online=False), lambda: attn(tq=512, online=True))
    x1 = _merge(x2d, o_n.reshape(b * s, d), proj, wa, wc, wm, wo, tm=1024)
    x2 = _mlp(x1, row(norm_mlp_g), w1, w2, tm=1024, tf=1024)
    return x2.reshape(b, s, d)


def kernel(x, mem, norm_mix_g, norm_mem_g, w_in, b_gate, q_norm_g, k_norm_g, lam_q1, lam_k1,
           lam_q2, lam_k2, subln_g, w_attn_o, conv_w, w_conv_o, w_mem_kv, mq_norm_g, mk_norm_g,
           w_mem_o, w_o, norm_mlp_g, w_mlp_in, w_mlp_out):
    depth = w_in.shape[0]
    for layer in range(depth):
        lam_init = 0.8 - 0.6 * float(np.exp(-0.3 * layer))
        x = _layer(x, mem, lam_init, norm_mix_g[layer], norm_mem_g[layer], w_in[layer],
                   b_gate[layer], q_norm_g[layer], k_norm_g[layer], lam_q1[layer], lam_k1[layer],
                   lam_q2[layer], lam_k2[layer], subln_g[layer], w_attn_o[layer], conv_w[layer],
                   w_conv_o[layer], w_mem_kv[layer], mq_norm_g[layer], mk_norm_g[layer],
                   w_mem_o[layer], w_o[layer], norm_mlp_g[layer], w_mlp_in[layer],
                   w_mlp_out[layer])
    return x
```

```python
import functools

import jax
import jax.numpy as jnp
import numpy as np
from jax import lax
from jax.experimental import pallas as pl
from jax.experimental.pallas import tpu as pltpu

F32 = jnp.float32
BF16 = jnp.bfloat16

NORM_EPS = 1e-6
MASK_VALUE = -1e30
LOG2_E = 1.4426950408889634
MAX_UNSHIFTED_LOGIT = 60.0

DIFF_HEADS = 8
DIFF_HEAD_DIM = 64
DIFF_V_DIM = 2 * DIFF_HEAD_DIM
MEM_HEADS = 4
MEM_HEAD_DIM = 256
CONV_K = 3

MXU_EDGE = 256
HALO = 8

W_Q, W_K, W_V, W_XC, W_GB, W_GC, W_QM, W_GATE = 0, 1, 2, 3, 4, 5, 6, 7
COL_Q, COL_K, COL_V, COL_YC, COL_OM, COL_GATE = 0, 1, 2, 3, 4, 5
N_COL_BLOCKS = 8

VMEM_LIMIT = 56 * 1024 * 1024

TM_IN_PROJ = 512
TM_MERGE = 1024
TM_MLP = 1024
TF_MLP = 1024
TQ_PLAIN = 1024
TQ_ONLINE = 512


def _rms(x, g):
    ms = jnp.mean(x * x, axis=-1, keepdims=True)
    return x * lax.rsqrt(ms + NORM_EPS) * g


def _dot(a, b):
    return jnp.dot(a, b, preferred_element_type=F32)


def _dot_nt(a, b):
    return lax.dot_general(a, b, (((1,), (1,)), ((), ())), preferred_element_type=F32)


def _mem_kv_kernel(mem_ref, g_ref, w_ref, kg_ref, kt_ref, v_ref):
    mem_n = _rms(mem_ref[0], g_ref[...]).astype(BF16)
    kv = _dot(mem_n, w_ref[...])
    width = kt_ref.shape[1]
    for h in range(MEM_HEADS):
        sl = slice(h * MEM_HEAD_DIM, (h + 1) * MEM_HEAD_DIM)
        kn = _rms(kv[:, sl], kg_ref[...])
        kt_ref[0, sl, :] = kn.T.astype(BF16)
    v_ref[0] = kv[:, width:].astype(BF16)


def _mem_kv(mem, g, w_kv, kg):
    b, m, d = mem.shape
    width = w_kv.shape[1] // 2
    return pl.pallas_call(
        _mem_kv_kernel,
        grid=(b,),
        in_specs=[
            pl.BlockSpec((1, m, d), lambda i: (i, 0, 0)),
            pl.BlockSpec((1, d), lambda i: (0, 0)),
            pl.BlockSpec((d, 2 * width), lambda i: (0, 0)),
            pl.BlockSpec((1, MEM_HEAD_DIM), lambda i: (0, 0)),
        ],
        out_specs=[
            pl.BlockSpec((1, width, m), lambda i: (i, 0, 0)),
            pl.BlockSpec((1, m, width), lambda i: (i, 0, 0)),
        ],
        out_shape=[
            jax.ShapeDtypeStruct((b, width, m), BF16),
            jax.ShapeDtypeStruct((b, m, width), BF16),
        ],
        compiler_params=pltpu.CompilerParams(
            dimension_semantics=("parallel",), vmem_limit_bytes=VMEM_LIMIT),
        name="mem_kv",
    )(mem, g, w_kv, kg)


def _in_proj_kernel(x_ref, g_ref, w_ref, gn_ref, cw_ref, bg_ref, kt_ref, vm_ref, *refs,
                    tiles_per_seq):
    n_cast = (len(refs) - 2) // 2
    o_ref, inner_scr = refs[n_cast], refs[-1]
    for w32_ref, w16_ref in zip(refs[:n_cast], refs[n_cast + 1:-1]):
        w16_ref[...] = w32_ref[...].astype(BF16)
    i = pl.program_id(0)
    tm, d = x_ref.shape

    @pl.when(i % tiles_per_seq == 0)
    def _():
        inner_scr[0:HALO, :] = jnp.zeros((HALO, d), F32)

    h = _rms(x_ref[...], g_ref[...]).astype(BF16)

    def proj(c):
        return _dot(h, w_ref[:, c * d:(c + 1) * d])

    def put(c, val):
        o_ref[:, c * d:(c + 1) * d] = val.astype(BF16)

    def gates():
        for t in range(3):
            z = proj(W_GATE + t) + bg_ref[:, t * d:(t + 1) * d]
            put(COL_GATE + t, 0.5 + 0.5 * jnp.tanh(0.5 * z))

    def conv():
        inner = proj(W_GC) * proj(W_XC)
        inner_scr[HALO:HALO + tm, :] = inner
        taps = (cw_ref[2:3, :] * inner
                + cw_ref[1:2, :] * inner_scr[HALO - 1:HALO - 1 + tm, :]
                + cw_ref[0:1, :] * inner_scr[HALO - 2:HALO - 2 + tm, :])
        put(COL_YC, proj(W_GB) * taps)
        inner_scr[0:HALO, :] = inner_scr[tm:tm + HALO, :]

    def group_normed(slot, src, dst):
        acc = proj(src)
        lane = lax.broadcasted_iota(jnp.int32, (tm, 2 * DIFF_HEAD_DIM), 1)
        low = lane < DIFF_HEAD_DIM
        for c in range(d // (2 * DIFF_HEAD_DIM)):
            sl = slice(c * 2 * DIFF_HEAD_DIM, (c + 1) * 2 * DIFF_HEAD_DIM)
            a = acc[:, sl]
            sq = a * a
            s_lo = jnp.sum(jnp.where(low, sq, 0.0), axis=-1, keepdims=True)
            s_hi = jnp.sum(jnp.where(low, 0.0, sq), axis=-1, keepdims=True)
            ms = jnp.where(low, s_lo, s_hi) * (1.0 / DIFF_HEAD_DIM)
            y = a * lax.rsqrt(ms + NORM_EPS) * gn_ref[slot, :, sl]
            o_ref[:, dst * d + sl.start:dst * d + sl.stop] = y.astype(BF16)

    def memory_attention():
        acc = proj(W_QM)
        for hd in range(MEM_HEADS):
            sl = slice(hd * MEM_HEAD_DIM, (hd + 1) * MEM_HEAD_DIM)
            qn = _rms(acc[:, sl], gn_ref[2, :, sl]).astype(BF16)
            s = _dot(qn, kt_ref[0, sl, :])
            p = jnp.exp(s - jnp.max(s, axis=-1, keepdims=True))
            p = p / jnp.sum(p, axis=-1, keepdims=True)
            o_ref[:, COL_OM * d + sl.start:COL_OM * d + sl.stop] = _dot(
                p.astype(BF16), vm_ref[0, :, sl]).astype(BF16)

    memory_attention()
    group_normed(1, W_K, COL_K)
    group_normed(0, W_Q, COL_Q)
    gates()
    conv()
    put(COL_V, proj(W_V))


def _resident(shape):
    return pl.BlockSpec(shape, lambda i: (0,) * len(shape), pipeline_mode=pl.Buffered(1))


def _in_proj(x2d, g, w_in, gn, conv_w, b_gate, kt, vm, later_weights, tm, seq):
    m, d = x2d.shape
    tiles = m // tm
    tiles_per_seq = seq // tm
    chunks = [pl.BlockSpec((w.shape[0] // tiles, w.shape[1]), lambda i: (i, 0))
              for w in later_weights]
    outs = pl.pallas_call(
        functools.partial(_in_proj_kernel, tiles_per_seq=tiles_per_seq),
        grid=(tiles,),
        in_specs=[
            pl.BlockSpec((tm, d), lambda i: (i, 0)),
            _resident((1, d)), _resident(w_in.shape),
            _resident(gn.shape), _resident(conv_w.shape), _resident(b_gate.shape),
            pl.BlockSpec((1,) + kt.shape[1:], lambda i: (i // tiles_per_seq, 0, 0)),
            pl.BlockSpec((1,) + vm.shape[1:], lambda i: (i // tiles_per_seq, 0, 0)),
        ] + chunks,
        out_specs=[pl.BlockSpec((tm, N_COL_BLOCKS * d), lambda i: (i, 0))] + chunks,
        out_shape=[jax.ShapeDtypeStruct((m, N_COL_BLOCKS * d), BF16)]
        + [jax.ShapeDtypeStruct(w.shape, BF16) for w in later_weights],
        scratch_shapes=[pltpu.VMEM((HALO + tm, d), F32)],
        compiler_params=pltpu.CompilerParams(
            dimension_semantics=("arbitrary",), vmem_limit_bytes=VMEM_LIMIT),
        name="in_proj",
    )(x2d, g, w_in, gn, conv_w, b_gate, kt, vm, *later_weights)
    return outs[0], tuple(outs[1:])


def _stack_maps(q):
    lane = lax.broadcasted_iota(jnp.int32, q.shape, 1)
    zero = jnp.zeros_like(q)
    return jnp.concatenate([jnp.where(lane < DIFF_HEAD_DIM, q, zero),
                            jnp.where(lane >= DIFF_HEAD_DIM, q, zero)], axis=0)


def _causal_mask(s, n):
    row = lax.broadcasted_iota(jnp.int32, (n, n), 0)
    col = lax.broadcasted_iota(jnp.int32, (n, n), 1)
    keep = jnp.concatenate([col <= row] * (s.shape[0] // n), axis=0)
    return jnp.where(keep, s, MASK_VALUE)


def _diff_lambda(lq1_ref, lk1_ref, lq2_ref, lk2_ref, lam_init):
    return (jnp.exp(jnp.sum(lq1_ref[...] * lk1_ref[...], axis=-1, keepdims=True))
            - jnp.exp(jnp.sum(lq2_ref[...] * lk2_ref[...], axis=-1, keepdims=True))
            + lam_init)


def _diff_finalize(acc, l, lam, sg, lam_init):
    n = acc.shape[0] // 2
    o = acc[:n] / l[:n] - lam * (acc[n:] / l[n:])
    return (_rms(o, sg) * (1.0 - lam_init)).astype(BF16)


def _diff_attn_online_kernel(lq1_ref, lk1_ref, lq2_ref, lk2_ref, sg_ref, q_ref, k_ref, v_ref,
                             o_ref, *, tq, lam_init):
    i = pl.program_id(2)
    qs = _stack_maps(q_ref[0])

    def step(j, carry, masked):
        m, l, acc = carry
        start = pl.multiple_of(j * tq, tq)
        s = _dot_nt(qs, k_ref[0, pl.ds(start, tq), :])
        if masked:
            s = _causal_mask(s, tq)
        m_new = jnp.maximum(m, jnp.max(s, axis=-1, keepdims=True))
        alpha = jnp.exp2(m - m_new)
        p = jnp.exp2(s - m_new)
        l = alpha * l + jnp.sum(p, axis=-1, keepdims=True)
        acc = alpha * acc + _dot(p.astype(BF16), v_ref[0, pl.ds(start, tq), :])
        return m_new, l, acc

    init = (jnp.full((2 * tq, 1), MASK_VALUE, F32), jnp.zeros((2 * tq, 1), F32),
            jnp.zeros((2 * tq, DIFF_V_DIM), F32))
    carry = lax.fori_loop(0, i, functools.partial(step, masked=False), init)
    _, l, acc = step(i, carry, masked=True)
    lam = _diff_lambda(lq1_ref, lk1_ref, lq2_ref, lk2_ref, lam_init)
    o_ref[0] = _diff_finalize(acc, l, lam, sg_ref[...], lam_init)


def _diff_attn_plain_kernel(lq1_ref, lk1_ref, lq2_ref, lk2_ref, sg_ref, q_ref, k_ref, v_ref,
                            o_ref, *, tq, lam_init, sub=MXU_EDGE):
    seq = q_ref.shape[1]
    lam = _diff_lambda(lq1_ref, lk1_ref, lq2_ref, lk2_ref, lam_init)

    def pv(s, start, size):
        v = v_ref[0, start:start + size, :]
        v_ones = jnp.concatenate([v, jnp.ones_like(v)], axis=1)
        return _dot(jnp.exp2(s).astype(BF16), v_ones)

    for i in reversed(range(seq // tq)):
        r0 = i * tq
        qs = _stack_maps(q_ref[0, r0:r0 + tq, :])
        acc = None
        for c in range(tq // sub):
            n = tq - c * sub
            qs_c = qs if c == 0 else jnp.concatenate([qs[c * sub:tq], qs[tq + c * sub:]], axis=0)
            s = _dot_nt(qs_c, k_ref[0, r0 + c * sub:r0 + (c + 1) * sub, :])
            pieces = [_causal_mask(s[:sub], sub), s[sub:n],
                      _causal_mask(s[n:n + sub], sub), s[n + sub:]]
            s = jnp.concatenate([p for p in pieces if p.shape[0]], axis=0)
            part = pv(s, r0 + c * sub, sub)
            if c == 0:
                acc = part
            else:
                acc = jnp.concatenate([acc[:c * sub], acc[c * sub:tq] + part[:n],
                                       acc[tq:tq + c * sub], acc[tq + c * sub:] + part[n:]],
                                      axis=0)
        for j in range(i):
            acc = acc + pv(_dot_nt(qs, k_ref[0, j * tq:(j + 1) * tq, :]), j * tq, tq)
        o_ref[0, r0:r0 + tq, :] = _diff_finalize(acc[:, :DIFF_V_DIM], acc[:, DIFF_V_DIM:], lam,
                                                 sg_ref[...], lam_init)


def _diff_attn(proj3d, lq1, lk1, lq2, lk2, subln_g, lam_init, tq, online):
    b, s, _ = proj3d.shape
    rows = tq if online else s
    grid = (b, DIFF_HEADS, s // rows)
    vec = pl.BlockSpec((1, DIFF_HEAD_DIM), lambda bi, h, i: (0, 0))
    body = _diff_attn_online_kernel if online else _diff_attn_plain_kernel
    return pl.pallas_call(
        functools.partial(body, tq=tq, lam_init=lam_init),
        grid=grid,
        in_specs=[
            vec, vec, vec, vec,
            pl.BlockSpec((1, DIFF_V_DIM), lambda bi, h, i: (0, 0)),
            pl.BlockSpec((1, rows, DIFF_V_DIM), lambda bi, h, i: (bi, i, COL_Q * DIFF_HEADS + h)),
            pl.BlockSpec((1, s, DIFF_V_DIM), lambda bi, h, i: (bi, 0, COL_K * DIFF_HEADS + h)),
            pl.BlockSpec((1, s, DIFF_V_DIM), lambda bi, h, i: (bi, 0, COL_V * DIFF_HEADS + h)),
        ],
        out_specs=pl.BlockSpec((1, rows, DIFF_V_DIM), lambda bi, h, i: (bi, i, h)),
        out_shape=jax.ShapeDtypeStruct((b, s, DIFF_HEADS * DIFF_V_DIM), BF16),
        compiler_params=pltpu.CompilerParams(
            dimension_semantics=("parallel", "parallel", "parallel"),
            vmem_limit_bytes=VMEM_LIMIT),
        name="diff_attn_online" if online else "diff_attn",
    )(lq1, lk1, lq2, lk2, subln_g, proj3d, proj3d, proj3d)


def _merge_kernel(x_ref, on_ref, yc_ref, om_ref, ga_ref, gc_ref, gm_ref,
                  wa_ref, wc_ref, wm_ref, wo_ref, o_ref):
    ya = _dot(on_ref[...], wa_ref[...])
    yc = _dot(yc_ref[...], wc_ref[...])
    ym = _dot(om_ref[...], wm_ref[...])

    merged = (ga_ref[...].astype(F32) * ya + gc_ref[...].astype(F32) * yc
              + gm_ref[...].astype(F32) * ym)
    o_ref[...] = x_ref[...] + _dot(merged.astype(BF16), wo_ref[...])


def _merge(x2d, on2d, proj, wa, wc, wm, wo, tm):
    m, d = x2d.shape

    def col(c):
        return pl.BlockSpec((tm, d), lambda i: (i, c))

    return pl.pallas_call(
        _merge_kernel,
        grid=(m // tm,),
        in_specs=[
            col(0), col(0),
            col(COL_YC), col(COL_OM), col(COL_GATE), col(COL_GATE + 1), col(COL_GATE + 2),
            _resident((d, d)), _resident((d, d)), _resident((d, d)), _resident((d, d)),
        ],
        out_specs=col(0),
        out_shape=jax.ShapeDtypeStruct((m, d), F32),
        compiler_params=pltpu.CompilerParams(
            dimension_semantics=("parallel",), vmem_limit_bytes=VMEM_LIMIT),
        name="merge",
    )(x2d, on2d, proj, proj, proj, proj, proj, wa, wc, wm, wo)


def _mlp_kernel(x_ref, g_ref, w1_ref, w2_ref, o_ref, *, tf):
    x = x_ref[...]
    h = _rms(x, g_ref[...]).astype(BF16)
    acc = x
    for f in range(w1_ref.shape[1] // tf):
        u = jnp.square(jnp.maximum(_dot(h, w1_ref[:, f * tf:(f + 1) * tf]), 0.0))
        acc = acc + _dot(u.astype(BF16), w2_ref[f * tf:(f + 1) * tf, :])
    o_ref[...] = acc


def _mlp(x2d, g, w1, w2, tm, tf):
    m, d = x2d.shape
    return pl.pallas_call(
        functools.partial(_mlp_kernel, tf=tf),
        grid=(m // tm,),
        in_specs=[
            pl.BlockSpec((tm, d), lambda i: (i, 0)),
            _resident((1, d)), _resident(w1.shape), _resident(w2.shape),
        ],
        out_specs=pl.BlockSpec((tm, d), lambda i: (i, 0)),
        out_shape=jax.ShapeDtypeStruct((m, d), F32),
        compiler_params=pltpu.CompilerParams(
            dimension_semantics=("parallel",), vmem_limit_bytes=VMEM_LIMIT),
        name="mlp",
    )(x2d, g, w1, w2)


def _layer(x, mem, lam_init, norm_mix_g, norm_mem_g, w_in, b_gate, q_norm_g, k_norm_g,
           lam_q1, lam_k1, lam_q2, lam_k2, subln_g, w_attn_o, conv_w, w_conv_o, w_mem_kv,
           mq_norm_g, mk_norm_g, w_mem_o, w_o, norm_mlp_g, w_mlp_in, w_mlp_out):
    b, s, d = x.shape
    x2d = x.reshape(b * s, d)
    row = lambda v: v.reshape(1, -1)

    gn = jnp.stack([
        jnp.tile(q_norm_g * (LOG2_E * DIFF_HEAD_DIM ** -0.5), d // DIFF_HEAD_DIM),
        jnp.tile(k_norm_g, d // DIFF_HEAD_DIM),
        jnp.tile(mq_norm_g * MEM_HEAD_DIM ** -0.5, d // MEM_HEAD_DIM),
    ]).reshape(3, 1, d)

    kt, vm = _mem_kv(mem, row(norm_mem_g), w_mem_kv.astype(BF16), row(mk_norm_g))
    later = (w_attn_o, w_conv_o, w_mem_o, w_o, w_mlp_in, w_mlp_out)
    proj, (wa, wc, wm, wo, w1, w2) = _in_proj(
        x2d, row(norm_mix_g), w_in.astype(BF16), gn, conv_w, row(b_gate), kt, vm, later,
        tm=TM_IN_PROJ, seq=s)

    score_bound = DIFF_HEAD_DIM ** 0.5 * jnp.max(jnp.abs(q_norm_g)) * jnp.max(jnp.abs(k_norm_g))
    attn = functools.partial(_diff_attn, proj.reshape(b, s, -1), row(lam_q1), row(lam_k1),
                             row(lam_q2), row(lam_k2), row(subln_g), lam_init)
    o_n = lax.cond(score_bound <= MAX_UNSHIFTED_LOGIT,
                   lambda: attn(tq=TQ_PLAIN, online=False),
                   lambda: attn(tq=TQ_ONLINE, online=True))
    x1 = _merge(x2d, o_n.reshape(b * s, d), proj, wa, wc, wm, wo, tm=TM_MERGE)
    x2 = _mlp(x1, row(norm_mlp_g), w1, w2, tm=TM_MLP, tf=TF_MLP)
    return x2.reshape(b, s, d)


def kernel(x, mem, norm_mix_g, norm_mem_g, w_in, b_gate, q_norm_g, k_norm_g, lam_q1, lam_k1,
           lam_q2, lam_k2, subln_g, w_attn_o, conv_w, w_conv_o, w_mem_kv, mq_norm_g, mk_norm_g,
           w_mem_o, w_o, norm_mlp_g, w_mlp_in, w_mlp_out):
    depth = w_in.shape[0]
    for layer in range(depth):
        lam_init = 0.8 - 0.6 * float(np.exp(-0.3 * layer))
        x = _layer(x, mem, lam_init, norm_mix_g[layer], norm_mem_g[layer], w_in[layer],
                   b_gate[layer], q_norm_g[layer], k_norm_g[layer], lam_q1[layer], lam_k1[layer],
                   lam_q2[layer], lam_k2[layer], subln_g[layer], w_attn_o[layer], conv_w[layer],
                   w_conv_o[layer], w_mem_kv[layer], mq_norm_g[layer], mk_norm_g[layer],
                   w_mem_o[layer], w_o[layer], norm_mlp_g[layer], w_mlp_in[layer],
                   w_mlp_out[layer])
    return x
```

```python
import functools

import jax
import jax.numpy as jnp
import numpy as np
from jax import lax
from jax.experimental import pallas as pl
from jax.experimental.pallas import tpu as pltpu

F32 = jnp.float32
BF16 = jnp.bfloat16

NORM_EPS = 1e-6
MASK_VALUE = -1e30
LOG2_E = 1.4426950408889634
MAX_UNSHIFTED_LOGIT = 60.0

DIFF_HEADS = 8
DIFF_HEAD_DIM = 64
DIFF_V_DIM = 2 * DIFF_HEAD_DIM
MEM_HEADS = 4
MEM_HEAD_DIM = 256
CONV_K = 3

MXU_EDGE = 256
HALO = 8

W_Q, W_K, W_V, W_XC, W_GB, W_GC, W_QM, W_GATE = 0, 1, 2, 3, 4, 5, 6, 7
COL_Q, COL_K, COL_V, COL_YC, COL_OM, COL_GATE = 0, 1, 2, 3, 4, 5
N_COL_BLOCKS = 8

VMEM_LIMIT = 56 * 1024 * 1024

TM_IN_PROJ = 512
TM_MERGE = 1024
TM_MLP = 1024
TF_MLP = 1024
TQ_PLAIN = 1024
TQ_ONLINE = 512


def _rms(x, g):
    ms = jnp.mean(x * x, axis=-1, keepdims=True)
    return x * lax.rsqrt(ms + NORM_EPS) * g


def _dot(a, b):
    return jnp.dot(a, b, preferred_element_type=F32)


def _dot_nt(a, b):
    return lax.dot_general(a, b, (((1,), (1,)), ((), ())), preferred_element_type=F32)


def _resident(shape):
    return pl.BlockSpec(shape, lambda i: (0,) * len(shape), pipeline_mode=pl.Buffered(1))


def _mem_kv_kernel(mem_ref, g_ref, w_ref, kg_ref, win_ref, kt_ref, v_ref, win16_ref):
    win16_ref[...] = win_ref[...].astype(BF16)
    mem_n = _rms(mem_ref[0], g_ref[...]).astype(BF16)
    kv = _dot(mem_n, w_ref[...].astype(BF16))
    width = kt_ref.shape[1]
    for h in range(MEM_HEADS):
        sl = slice(h * MEM_HEAD_DIM, (h + 1) * MEM_HEAD_DIM)
        kn = _rms(kv[:, sl], kg_ref[...])
        kt_ref[0, sl, :] = kn.T.astype(BF16)
    v_ref[0] = kv[:, width:].astype(BF16)


def _mem_kv(mem, g, w_kv, kg, w_in):
    b, m, d = mem.shape
    width = w_kv.shape[1] // 2
    w_in_chunk = pl.BlockSpec((w_in.shape[0] // b, w_in.shape[1]), lambda i: (i, 0))
    return pl.pallas_call(
        _mem_kv_kernel,
        grid=(b,),
        in_specs=[
            pl.BlockSpec((1, m, d), lambda i: (i, 0, 0)),
            _resident((1, d)), _resident(w_kv.shape), _resident((1, MEM_HEAD_DIM)),
            w_in_chunk,
        ],
        out_specs=[
            pl.BlockSpec((1, width, m), lambda i: (i, 0, 0)),
            pl.BlockSpec((1, m, width), lambda i: (i, 0, 0)),
            w_in_chunk,
        ],
        out_shape=[
            jax.ShapeDtypeStruct((b, width, m), BF16),
            jax.ShapeDtypeStruct((b, m, width), BF16),
            jax.ShapeDtypeStruct(w_in.shape, BF16),
        ],
        compiler_params=pltpu.CompilerParams(
            dimension_semantics=("parallel",), vmem_limit_bytes=VMEM_LIMIT),
        name="mem_kv",
    )(mem, g, w_kv, kg, w_in)


def _in_proj_kernel(x_ref, g_ref, w_ref, gn_ref, cw_ref, bg_ref, kt_ref, vm_ref, *refs,
                    tiles_per_seq):
    n_cast = (len(refs) - 2) // 2
    o_ref, inner_scr = refs[n_cast], refs[-1]
    for w32_ref, w16_ref in zip(refs[:n_cast], refs[n_cast + 1:-1]):
        w16_ref[...] = w32_ref[...].astype(BF16)
    i = pl.program_id(0)
    tm, d = x_ref.shape

    @pl.when(i % tiles_per_seq == 0)
    def _():
        inner_scr[0:HALO, :] = jnp.zeros((HALO, d), F32)

    h = _rms(x_ref[...], g_ref[...]).astype(BF16)

    def proj(c):
        return _dot(h, w_ref[:, c * d:(c + 1) * d])

    def put(c, val):
        o_ref[:, c * d:(c + 1) * d] = val.astype(BF16)

    def gates():
        for t in range(3):
            z = proj(W_GATE + t) + bg_ref[:, t * d:(t + 1) * d]
            put(COL_GATE + t, 0.5 + 0.5 * jnp.tanh(0.5 * z))

    def conv():
        inner = proj(W_GC) * proj(W_XC)
        inner_scr[HALO:HALO + tm, :] = inner
        taps = (cw_ref[2:3, :] * inner
                + cw_ref[1:2, :] * inner_scr[HALO - 1:HALO - 1 + tm, :]
                + cw_ref[0:1, :] * inner_scr[HALO - 2:HALO - 2 + tm, :])
        put(COL_YC, proj(W_GB) * taps)
        inner_scr[0:HALO, :] = inner_scr[tm:tm + HALO, :]

    def group_normed(slot, src, dst):
        acc = proj(src)
        lane = lax.broadcasted_iota(jnp.int32, (tm, 2 * DIFF_HEAD_DIM), 1)
        low = lane < DIFF_HEAD_DIM
        for c in range(d // (2 * DIFF_HEAD_DIM)):
            sl = slice(c * 2 * DIFF_HEAD_DIM, (c + 1) * 2 * DIFF_HEAD_DIM)
            a = acc[:, sl]
            sq = a * a
            s_lo = jnp.sum(jnp.where(low, sq, 0.0), axis=-1, keepdims=True)
            s_hi = jnp.sum(jnp.where(low, 0.0, sq), axis=-1, keepdims=True)
            ms = jnp.where(low, s_lo, s_hi) * (1.0 / DIFF_HEAD_DIM)
            y = a * lax.rsqrt(ms + NORM_EPS) * gn_ref[slot, :, sl]
            o_ref[:, dst * d + sl.start:dst * d + sl.stop] = y.astype(BF16)

    def memory_attention():
        acc = proj(W_QM)
        for hd in range(MEM_HEADS):
            sl = slice(hd * MEM_HEAD_DIM, (hd + 1) * MEM_HEAD_DIM)
            qn = _rms(acc[:, sl], gn_ref[2, :, sl]).astype(BF16)
            s = _dot(qn, kt_ref[0, sl, :])
            p = jnp.exp(s - jnp.max(s, axis=-1, keepdims=True))
            p = p / jnp.sum(p, axis=-1, keepdims=True)
            o_ref[:, COL_OM * d + sl.start:COL_OM * d + sl.stop] = _dot(
                p.astype(BF16), vm_ref[0, :, sl]).astype(BF16)

    memory_attention()
    group_normed(1, W_K, COL_K)
    group_normed(0, W_Q, COL_Q)
    gates()
    conv()
    put(COL_V, proj(W_V))


def _in_proj(x2d, g, w_in, gn, conv_w, b_gate, kt, vm, later_weights, tm, seq):
    m, d = x2d.shape
    tiles = m // tm
    tiles_per_seq = seq // tm
    chunks = [pl.BlockSpec((w.shape[0] // tiles, w.shape[1]), lambda i: (i, 0))
              for w in later_weights]
    outs = pl.pallas_call(
        functools.partial(_in_proj_kernel, tiles_per_seq=tiles_per_seq),
        grid=(tiles,),
        in_specs=[
            pl.BlockSpec((tm, d), lambda i: (i, 0)),
            _resident((1, d)), _resident(w_in.shape),
            _resident(gn.shape), _resident(conv_w.shape), _resident(b_gate.shape),
            pl.BlockSpec((1,) + kt.shape[1:], lambda i: (i // tiles_per_seq, 0, 0)),
            pl.BlockSpec((1,) + vm.shape[1:], lambda i: (i // tiles_per_seq, 0, 0)),
        ] + chunks,
        out_specs=[pl.BlockSpec((tm, N_COL_BLOCKS * d), lambda i: (i, 0))] + chunks,
        out_shape=[jax.ShapeDtypeStruct((m, N_COL_BLOCKS * d), BF16)]
        + [jax.ShapeDtypeStruct(w.shape, BF16) for w in later_weights],
        scratch_shapes=[pltpu.VMEM((HALO + tm, d), F32)],
        compiler_params=pltpu.CompilerParams(
            dimension_semantics=("arbitrary",), vmem_limit_bytes=VMEM_LIMIT),
        name="in_proj",
    )(x2d, g, w_in, gn, conv_w, b_gate, kt, vm, *later_weights)
    return outs[0], tuple(outs[1:])


def _stack_maps(q):
    lane = lax.broadcasted_iota(jnp.int32, q.shape, 1)
    zero = jnp.zeros_like(q)
    return jnp.concatenate([jnp.where(lane < DIFF_HEAD_DIM, q, zero),
                            jnp.where(lane >= DIFF_HEAD_DIM, q, zero)], axis=0)


def _causal_mask(s, n):
    row = lax.broadcasted_iota(jnp.int32, (n, n), 0)
    col = lax.broadcasted_iota(jnp.int32, (n, n), 1)
    keep = jnp.concatenate([col <= row] * (s.shape[0] // n), axis=0)
    return jnp.where(keep, s, MASK_VALUE)


def _diff_lambda(lq1_ref, lk1_ref, lq2_ref, lk2_ref, lam_init):
    return (jnp.exp(jnp.sum(lq1_ref[...] * lk1_ref[...], axis=-1, keepdims=True))
            - jnp.exp(jnp.sum(lq2_ref[...] * lk2_ref[...], axis=-1, keepdims=True))
            + lam_init)


def _diff_finalize(acc, l, lam, sg, lam_init):
    n = acc.shape[0] // 2
    o = acc[:n] / l[:n] - lam * (acc[n:] / l[n:])
    return (_rms(o, sg) * (1.0 - lam_init)).astype(BF16)


def _diff_attn_online_kernel(lq1_ref, lk1_ref, lq2_ref, lk2_ref, sg_ref, q_ref, k_ref, v_ref,
                             o_ref, *, tq, lam_init):
    i = pl.program_id(2)
    qs = _stack_maps(q_ref[0])

    def step(j, carry, masked):
        m, l, acc = carry
        start = pl.multiple_of(j * tq, tq)
        s = _dot_nt(qs, k_ref[0, pl.ds(start, tq), :])
        if masked:
            s = _causal_mask(s, tq)
        m_new = jnp.maximum(m, jnp.max(s, axis=-1, keepdims=True))
        alpha = jnp.exp2(m - m_new)
        p = jnp.exp2(s - m_new)
        l = alpha * l + jnp.sum(p, axis=-1, keepdims=True)
        acc = alpha * acc + _dot(p.astype(BF16), v_ref[0, pl.ds(start, tq), :])
        return m_new, l, acc

    init = (jnp.full((2 * tq, 1), MASK_VALUE, F32), jnp.zeros((2 * tq, 1), F32),
            jnp.zeros((2 * tq, DIFF_V_DIM), F32))
    carry = lax.fori_loop(0, i, functools.partial(step, masked=False), init)
    _, l, acc = step(i, carry, masked=True)
    lam = _diff_lambda(lq1_ref, lk1_ref, lq2_ref, lk2_ref, lam_init)
    o_ref[0] = _diff_finalize(acc, l, lam, sg_ref[...], lam_init)


def _diff_attn_plain_kernel(lq1_ref, lk1_ref, lq2_ref, lk2_ref, sg_ref, q_ref, k_ref, v_ref,
                            o_ref, *, tq, lam_init, sub=MXU_EDGE):
    seq = q_ref.shape[1]
    lam = _diff_lambda(lq1_ref, lk1_ref, lq2_ref, lk2_ref, lam_init)

    def pv(s, start, size):
        v = v_ref[0, start:start + size, :]
        v_ones = jnp.concatenate([v, jnp.ones_like(v)], axis=1)
        return _dot(jnp.exp2(s).astype(BF16), v_ones)

    for i in reversed(range(seq // tq)):
        r0 = i * tq
        qs = _stack_maps(q_ref[0, r0:r0 + tq, :])
        acc = None
        for c in range(tq // sub):
            n = tq - c * sub
            qs_c = qs if c == 0 else jnp.concatenate([qs[c * sub:tq], qs[tq + c * sub:]], axis=0)
            s = _dot_nt(qs_c, k_ref[0, r0 + c * sub:r0 + (c + 1) * sub, :])
            pieces = [_causal_mask(s[:sub], sub), s[sub:n],
                      _causal_mask(s[n:n + sub], sub), s[n + sub:]]
            s = jnp.concatenate([p for p in pieces if p.shape[0]], axis=0)
            part = pv(s, r0 + c * sub, sub)
            if c == 0:
                acc = part
            else:
                acc = jnp.concatenate([acc[:c * sub], acc[c * sub:tq] + part[:n],
                                       acc[tq:tq + c * sub], acc[tq + c * sub:] + part[n:]],
                                      axis=0)
        for j in range(i):
            acc = acc + pv(_dot_nt(qs, k_ref[0, j * tq:(j + 1) * tq, :]), j * tq, tq)
        o_ref[0, r0:r0 + tq, :] = _diff_finalize(acc[:, :DIFF_V_DIM], acc[:, DIFF_V_DIM:], lam,
                                                 sg_ref[...], lam_init)


def _diff_attn(proj3d, lq1, lk1, lq2, lk2, subln_g, lam_init, tq, online):
    b, s, _ = proj3d.shape
    rows = tq if online else s
    grid = (b, DIFF_HEADS, s // rows)
    vec = pl.BlockSpec((1, DIFF_HEAD_DIM), lambda bi, h, i: (0, 0))
    body = _diff_attn_online_kernel if online else _diff_attn_plain_kernel
    return pl.pallas_call(
        functools.partial(body, tq=tq, lam_init=lam_init),
        grid=grid,
        in_specs=[
            vec, vec, vec, vec,
            pl.BlockSpec((1, DIFF_V_DIM), lambda bi, h, i: (0, 0)),
            pl.BlockSpec((1, rows, DIFF_V_DIM), lambda bi, h, i: (bi, i, COL_Q * DIFF_HEADS + h)),
            pl.BlockSpec((1, s, DIFF_V_DIM), lambda bi, h, i: (bi, 0, COL_K * DIFF_HEADS + h)),
            pl.BlockSpec((1, s, DIFF_V_DIM), lambda bi, h, i: (bi, 0, COL_V * DIFF_HEADS + h)),
        ],
        out_specs=pl.BlockSpec((1, rows, DIFF_V_DIM), lambda bi, h, i: (bi, i, h)),
        out_shape=jax.ShapeDtypeStruct((b, s, DIFF_HEADS * DIFF_V_DIM), BF16),
        compiler_params=pltpu.CompilerParams(
            dimension_semantics=("parallel", "parallel", "parallel"),
            vmem_limit_bytes=VMEM_LIMIT),
        name="diff_attn_online" if online else "diff_attn",
    )(lq1, lk1, lq2, lk2, subln_g, proj3d, proj3d, proj3d)


def _merge_kernel(x_ref, on_ref, yc_ref, om_ref, ga_ref, gc_ref, gm_ref,
                  wa_ref, wc_ref, wm_ref, wo_ref, o_ref):
    ya = _dot(on_ref[...], wa_ref[...])
    yc = _dot(yc_ref[...], wc_ref[...])
    ym = _dot(om_ref[...], wm_ref[...])

    merged = (ga_ref[...].astype(F32) * ya + gc_ref[...].astype(F32) * yc
              + gm_ref[...].astype(F32) * ym)
    o_ref[...] = x_ref[...] + _dot(merged.astype(BF16), wo_ref[...])


def _merge(x2d, on2d, proj, wa, wc, wm, wo, tm):
    m, d = x2d.shape

    def col(c):
        return pl.BlockSpec((tm, d), lambda i: (i, c))

    return pl.pallas_call(
        _merge_kernel,
        grid=(m // tm,),
        in_specs=[
            col(0), col(0),
            col(COL_YC), col(COL_OM), col(COL_GATE), col(COL_GATE + 1), col(COL_GATE + 2),
            _resident((d, d)), _resident((d, d)), _resident((d, d)), _resident((d, d)),
        ],
        out_specs=col(0),
        out_shape=jax.ShapeDtypeStruct((m, d), F32),
        compiler_params=pltpu.CompilerParams(
            dimension_semantics=("parallel",), vmem_limit_bytes=VMEM_LIMIT),
        name="merge",
    )(x2d, on2d, proj, proj, proj, proj, proj, wa, wc, wm, wo)


def _mlp_kernel(x_ref, g_ref, w1_ref, w2_ref, o_ref, *, tf):
    x = x_ref[...]
    h = _rms(x, g_ref[...]).astype(BF16)
    acc = x
    for f in range(w1_ref.shape[1] // tf):
        u = jnp.square(jnp.maximum(_dot(h, w1_ref[:, f * tf:(f + 1) * tf]), 0.0))
        acc = acc + _dot(u.astype(BF16), w2_ref[f * tf:(f + 1) * tf, :])
    o_ref[...] = acc


def _mlp(x2d, g, w1, w2, tm, tf):
    m, d = x2d.shape
    return pl.pallas_call(
        functools.partial(_mlp_kernel, tf=tf),
        grid=(m // tm,),
        in_specs=[
            pl.BlockSpec((tm, d), lambda i: (i, 0)),
            _resident((1, d)), _resident(w1.shape), _resident(w2.shape),
        ],
        out_specs=pl.BlockSpec((tm, d), lambda i: (i, 0)),
        out_shape=jax.ShapeDtypeStruct((m, d), F32),
        compiler_params=pltpu.CompilerParams(
            dimension_semantics=("parallel",), vmem_limit_bytes=VMEM_LIMIT),
        name="mlp",
    )(x2d, g, w1, w2)


def _layer(x, mem, lam_init, norm_mix_g, norm_mem_g, w_in, b_gate, q_norm_g, k_norm_g,
           lam_q1, lam_k1, lam_q2, lam_k2, subln_g, w_attn_o, conv_w, w_conv_o, w_mem_kv,
           mq_norm_g, mk_norm_g, w_mem_o, w_o, norm_mlp_g, w_mlp_in, w_mlp_out):
    b, s, d = x.shape
    x2d = x.reshape(b * s, d)
    row = lambda v: v.reshape(1, -1)

    gn = jnp.stack([
        jnp.tile(q_norm_g * (LOG2_E * DIFF_HEAD_DIM ** -0.5), d // DIFF_HEAD_DIM),
        jnp.tile(k_norm_g, d // DIFF_HEAD_DIM),
        jnp.tile(mq_norm_g * MEM_HEAD_DIM ** -0.5, d // MEM_HEAD_DIM),
    ]).reshape(3, 1, d)

    kt, vm, w_in16 = _mem_kv(mem, row(norm_mem_g), w_mem_kv, row(mk_norm_g), w_in)
    later = (w_attn_o, w_conv_o, w_mem_o, w_o, w_mlp_in, w_mlp_out)
    proj, (wa, wc, wm, wo, w1, w2) = _in_proj(
        x2d, row(norm_mix_g), w_in16, gn, conv_w, row(b_gate), kt, vm, later,
        tm=TM_IN_PROJ, seq=s)

    score_bound = DIFF_HEAD_DIM ** 0.5 * jnp.max(jnp.abs(q_norm_g)) * jnp.max(jnp.abs(k_norm_g))
    attn = functools.partial(_diff_attn, proj.reshape(b, s, -1), row(lam_q1), row(lam_k1),
                             row(lam_q2), row(lam_k2), row(subln_g), lam_init)
    o_n = lax.cond(score_bound <= MAX_UNSHIFTED_LOGIT,
                   lambda: attn(tq=TQ_PLAIN, online=False),
                   lambda: attn(tq=TQ_ONLINE, online=True))
    x1 = _merge(x2d, o_n.reshape(b * s, d), proj, wa, wc, wm, wo, tm=TM_MERGE)
    x2 = _mlp(x1, row(norm_mlp_g), w1, w2, tm=TM_MLP, tf=TF_MLP)
    return x2.reshape(b, s, d)


def kernel(x, mem, norm_mix_g, norm_mem_g, w_in, b_gate, q_norm_g, k_norm_g, lam_q1, lam_k1,
           lam_q2, lam_k2, subln_g, w_attn_o, conv_w, w_conv_o, w_mem_kv, mq_norm_g, mk_norm_g,
           w_mem_o, w_o, norm_mlp_g, w_mlp_in, w_mlp_out):
    depth = w_in.shape[0]
    for layer in range(depth):
        lam_init = 0.8 - 0.6 * float(np.exp(-0.3 * layer))
        x = _layer(x, mem, lam_init, norm_mix_g[layer], norm_mem_g[layer], w_in[layer],
                   b_gate[layer], q_norm_g[layer], k_norm_g[layer], lam_q1[layer], lam_k1[layer],
                   lam_q2[layer], lam_k2[layer], subln_g[layer], w_attn_o[layer], conv_w[layer],
                   w_conv_o[layer], w_mem_kv[layer], mq_norm_g[layer], mk_norm_g[layer],
                   w_mem_o[layer], w_o[layer], norm_mlp_g[layer], w_mlp_in[layer],
                   w_mlp_out[layer])
    return x
```

```python
import functools

import jax
import jax.numpy as jnp
import numpy as np
from jax import lax
from jax.experimental import pallas as pl
from jax.experimental.pallas import tpu as pltpu

F32 = jnp.float32
BF16 = jnp.bfloat16

NORM_EPS = 1e-6
MASK_VALUE = -1e30
LOG2_E = 1.4426950408889634
MAX_UNSHIFTED_LOGIT = 60.0

DIFF_HEADS = 8
DIFF_HEAD_DIM = 64
DIFF_V_DIM = 2 * DIFF_HEAD_DIM
MEM_HEADS = 4
MEM_HEAD_DIM = 256
CONV_K = 3

MXU_EDGE = 256
HALO = 8

W_Q, W_K, W_V, W_XC, W_GB, W_GC, W_QM, W_GATE = 0, 1, 2, 3, 4, 5, 6, 7
COL_Q, COL_K, COL_V, COL_YC, COL_OM, COL_GATE = 0, 1, 2, 3, 4, 5
N_COL_BLOCKS = 8

VMEM_LIMIT = 56 * 1024 * 1024

TM_IN_PROJ = 512
TM_MERGE = 1024
TM_MLP = 1024
TF_MLP = 1024
TQ_PLAIN = 1024
TQ_ONLINE = 512
HEADS_PER_STEP = 2


def _rms(x, g):
    ms = jnp.mean(x * x, axis=-1, keepdims=True)
    return x * lax.rsqrt(ms + NORM_EPS) * g


def _dot(a, b):
    return jnp.dot(a, b, preferred_element_type=F32)


def _dot_nt(a, b):
    return lax.dot_general(a, b, (((1,), (1,)), ((), ())), preferred_element_type=F32)


def _resident(shape):
    return pl.BlockSpec(shape, lambda i: (0,) * len(shape), pipeline_mode=pl.Buffered(1))


def _mem_kv_kernel(mem_ref, g_ref, w_ref, kg_ref, win_ref, kt_ref, v_ref, win16_ref):
    win16_ref[...] = win_ref[...].astype(BF16)
    mem_n = _rms(mem_ref[0], g_ref[...]).astype(BF16)
    kv = _dot(mem_n, w_ref[...].astype(BF16))
    width = kt_ref.shape[1]
    for h in range(MEM_HEADS):
        sl = slice(h * MEM_HEAD_DIM, (h + 1) * MEM_HEAD_DIM)
        kn = _rms(kv[:, sl], kg_ref[...])
        kt_ref[0, sl, :] = kn.T.astype(BF16)
    v_ref[0] = kv[:, width:].astype(BF16)


def _mem_kv(mem, g, w_kv, kg, w_in):
    b, m, d = mem.shape
    width = w_kv.shape[1] // 2
    w_in_chunk = pl.BlockSpec((w_in.shape[0] // b, w_in.shape[1]), lambda i: (i, 0))
    return pl.pallas_call(
        _mem_kv_kernel,
        grid=(b,),
        in_specs=[
            pl.BlockSpec((1, m, d), lambda i: (i, 0, 0)),
            _resident((1, d)), _resident(w_kv.shape), _resident((1, MEM_HEAD_DIM)),
            w_in_chunk,
        ],
        out_specs=[
            pl.BlockSpec((1, width, m), lambda i: (i, 0, 0)),
            pl.BlockSpec((1, m, width), lambda i: (i, 0, 0)),
            w_in_chunk,
        ],
        out_shape=[
            jax.ShapeDtypeStruct((b, width, m), BF16),
            jax.ShapeDtypeStruct((b, m, width), BF16),
            jax.ShapeDtypeStruct(w_in.shape, BF16),
        ],
        compiler_params=pltpu.CompilerParams(
            dimension_semantics=("parallel",), vmem_limit_bytes=VMEM_LIMIT),
        name="mem_kv",
    )(mem, g, w_kv, kg, w_in)


def _in_proj_kernel(x_ref, g_ref, w_ref, gn_ref, cw_ref, bg_ref, kt_ref, vm_ref, *refs,
                    tiles_per_seq):
    n_cast = (len(refs) - 2) // 2
    o_ref, inner_scr = refs[n_cast], refs[-1]
    for w32_ref, w16_ref in zip(refs[:n_cast], refs[n_cast + 1:-1]):
        w16_ref[...] = w32_ref[...].astype(BF16)
    i = pl.program_id(0)
    tm, d = x_ref.shape

    @pl.when(i % tiles_per_seq == 0)
    def _():
        inner_scr[0:HALO, :] = jnp.zeros((HALO, d), F32)

    h = _rms(x_ref[...], g_ref[...]).astype(BF16)

    def proj(c):
        return _dot(h, w_ref[:, c * d:(c + 1) * d])

    def put(c, val):
        o_ref[:, c * d:(c + 1) * d] = val.astype(BF16)

    def gates():
        for t in range(3):
            z = proj(W_GATE + t) + bg_ref[:, t * d:(t + 1) * d]
            put(COL_GATE + t, 0.5 + 0.5 * jnp.tanh(0.5 * z))

    def conv():
        inner = proj(W_GC) * proj(W_XC)
        inner_scr[HALO:HALO + tm, :] = inner
        taps = (cw_ref[2:3, :] * inner
                + cw_ref[1:2, :] * inner_scr[HALO - 1:HALO - 1 + tm, :]
                + cw_ref[0:1, :] * inner_scr[HALO - 2:HALO - 2 + tm, :])
        put(COL_YC, proj(W_GB) * taps)
        inner_scr[0:HALO, :] = inner_scr[tm:tm + HALO, :]

    def group_normed(slot, src, dst):
        acc = proj(src)
        lane = lax.broadcasted_iota(jnp.int32, (tm, 2 * DIFF_HEAD_DIM), 1)
        low = lane < DIFF_HEAD_DIM
        for c in range(d // (2 * DIFF_HEAD_DIM)):
            sl = slice(c * 2 * DIFF_HEAD_DIM, (c + 1) * 2 * DIFF_HEAD_DIM)
            a = acc[:, sl]
            sq = a * a
            s_lo = jnp.sum(jnp.where(low, sq, 0.0), axis=-1, keepdims=True)
            s_hi = jnp.sum(jnp.where(low, 0.0, sq), axis=-1, keepdims=True)
            ms = jnp.where(low, s_lo, s_hi) * (1.0 / DIFF_HEAD_DIM)
            y = a * lax.rsqrt(ms + NORM_EPS) * gn_ref[slot, :, sl]
            o_ref[:, dst * d + sl.start:dst * d + sl.stop] = y.astype(BF16)

    def memory_attention():
        acc = proj(W_QM)
        for hd in range(MEM_HEADS):
            sl = slice(hd * MEM_HEAD_DIM, (hd + 1) * MEM_HEAD_DIM)
            qn = _rms(acc[:, sl], gn_ref[2, :, sl]).astype(BF16)
            s = _dot(qn, kt_ref[0, sl, :])
            p = jnp.exp(s - jnp.max(s, axis=-1, keepdims=True))
            p = p / jnp.sum(p, axis=-1, keepdims=True)
            o_ref[:, COL_OM * d + sl.start:COL_OM * d + sl.stop] = _dot(
                p.astype(BF16), vm_ref[0, :, sl]).astype(BF16)

    memory_attention()
    group_normed(1, W_K, COL_K)
    group_normed(0, W_Q, COL_Q)
    gates()
    conv()
    put(COL_V, proj(W_V))


def _in_proj(x2d, g, w_in, gn, conv_w, b_gate, kt, vm, later_weights, tm, seq):
    m, d = x2d.shape
    tiles = m // tm
    tiles_per_seq = seq // tm
    chunks = [pl.BlockSpec((w.shape[0] // tiles, w.shape[1]), lambda i: (i, 0))
              for w in later_weights]
    outs = pl.pallas_call(
        functools.partial(_in_proj_kernel, tiles_per_seq=tiles_per_seq),
        grid=(tiles,),
        in_specs=[
            pl.BlockSpec((tm, d), lambda i: (i, 0)),
            _resident((1, d)), _resident(w_in.shape),
            _resident(gn.shape), _resident(conv_w.shape), _resident(b_gate.shape),
            pl.BlockSpec((1,) + kt.shape[1:], lambda i: (i // tiles_per_seq, 0, 0)),
            pl.BlockSpec((1,) + vm.shape[1:], lambda i: (i // tiles_per_seq, 0, 0)),
        ] + chunks,
        out_specs=[pl.BlockSpec((tm, N_COL_BLOCKS * d), lambda i: (i, 0))] + chunks,
        out_shape=[jax.ShapeDtypeStruct((m, N_COL_BLOCKS * d), BF16)]
        + [jax.ShapeDtypeStruct(w.shape, BF16) for w in later_weights],
        scratch_shapes=[pltpu.VMEM((HALO + tm, d), F32)],
        compiler_params=pltpu.CompilerParams(
            dimension_semantics=("arbitrary",), vmem_limit_bytes=VMEM_LIMIT),
        name="in_proj",
    )(x2d, g, w_in, gn, conv_w, b_gate, kt, vm, *later_weights)
    return outs[0], tuple(outs[1:])


def _stack_maps(q):
    lane = lax.broadcasted_iota(jnp.int32, q.shape, 1)
    zero = jnp.zeros_like(q)
    return jnp.concatenate([jnp.where(lane < DIFF_HEAD_DIM, q, zero),
                            jnp.where(lane >= DIFF_HEAD_DIM, q, zero)], axis=0)


def _causal_mask(s, n):
    row = lax.broadcasted_iota(jnp.int32, (n, n), 0)
    col = lax.broadcasted_iota(jnp.int32, (n, n), 1)
    keep = jnp.concatenate([col <= row] * (s.shape[0] // n), axis=0)
    return jnp.where(keep, s, MASK_VALUE)


def _diff_lambda(lq1_ref, lk1_ref, lq2_ref, lk2_ref, lam_init):
    return (jnp.exp(jnp.sum(lq1_ref[...] * lk1_ref[...], axis=-1, keepdims=True))
            - jnp.exp(jnp.sum(lq2_ref[...] * lk2_ref[...], axis=-1, keepdims=True))
            + lam_init)


def _diff_finalize(acc, l, lam, sg, lam_init):
    n = acc.shape[0] // 2
    o = acc[:n] / l[:n] - lam * (acc[n:] / l[n:])
    return (_rms(o, sg) * (1.0 - lam_init)).astype(BF16)


def _diff_attn_online_kernel(lq1_ref, lk1_ref, lq2_ref, lk2_ref, sg_ref, q_ref, k_ref, v_ref,
                             o_ref, *, tq, lam_init):
    i = pl.program_id(2)
    qs = _stack_maps(q_ref[0])

    def step(j, carry, masked):
        m, l, acc = carry
        start = pl.multiple_of(j * tq, tq)
        s = _dot_nt(qs, k_ref[0, pl.ds(start, tq), :])
        if masked:
            s = _causal_mask(s, tq)
        m_new = jnp.maximum(m, jnp.max(s, axis=-1, keepdims=True))
        alpha = jnp.exp2(m - m_new)
        p = jnp.exp2(s - m_new)
        l = alpha * l + jnp.sum(p, axis=-1, keepdims=True)
        acc = alpha * acc + _dot(p.astype(BF16), v_ref[0, pl.ds(start, tq), :])
        return m_new, l, acc

    init = (jnp.full((2 * tq, 1), MASK_VALUE, F32), jnp.zeros((2 * tq, 1), F32),
            jnp.zeros((2 * tq, DIFF_V_DIM), F32))
    carry = lax.fori_loop(0, i, functools.partial(step, masked=False), init)
    _, l, acc = step(i, carry, masked=True)
    lam = _diff_lambda(lq1_ref, lk1_ref, lq2_ref, lk2_ref, lam_init)
    o_ref[0] = _diff_finalize(acc, l, lam, sg_ref[...], lam_init)


def _diff_attn_plain_block(q_ref, k_ref, v_ref, o_ref, sg, lam, i, lanes, *, tq, sub, lam_init):
    def pv(s, start, size):
        v = v_ref[0, start:start + size, lanes]
        v_ones = jnp.concatenate([v, jnp.ones_like(v)], axis=1)
        return _dot(jnp.exp2(s).astype(BF16), v_ones)

    r0 = i * tq
    qs = _stack_maps(q_ref[0, r0:r0 + tq, lanes])
    acc = None
    for c in range(tq // sub):
        n = tq - c * sub
        qs_c = qs if c == 0 else jnp.concatenate([qs[c * sub:tq], qs[tq + c * sub:]], axis=0)
        s = _dot_nt(qs_c, k_ref[0, r0 + c * sub:r0 + (c + 1) * sub, lanes])
        pieces = [_causal_mask(s[:sub], sub), s[sub:n],
                  _causal_mask(s[n:n + sub], sub), s[n + sub:]]
        s = jnp.concatenate([p for p in pieces if p.shape[0]], axis=0)
        part = pv(s, r0 + c * sub, sub)
        if c == 0:
            acc = part
        else:
            acc = jnp.concatenate([acc[:c * sub], acc[c * sub:tq] + part[:n],
                                   acc[tq:tq + c * sub], acc[tq + c * sub:] + part[n:]],
                                  axis=0)
    for j in range(i):
        acc = acc + pv(_dot_nt(qs, k_ref[0, j * tq:(j + 1) * tq, lanes]), j * tq, tq)
    o_ref[0, r0:r0 + tq, lanes] = _diff_finalize(acc[:, :DIFF_V_DIM], acc[:, DIFF_V_DIM:], lam,
                                                 sg, lam_init)


def _diff_attn_plain_kernel(lq1_ref, lk1_ref, lq2_ref, lk2_ref, sg_ref, q_ref, k_ref, v_ref,
                            o_ref, *, tq, lam_init, sub=MXU_EDGE):
    lam = _diff_lambda(lq1_ref, lk1_ref, lq2_ref, lk2_ref, lam_init)
    for i in reversed(range(q_ref.shape[1] // tq)):
        for hd in range(q_ref.shape[2] // DIFF_V_DIM):
            _diff_attn_plain_block(q_ref, k_ref, v_ref, o_ref, sg_ref[...], lam, i,
                                   slice(hd * DIFF_V_DIM, (hd + 1) * DIFF_V_DIM),
                                   tq=tq, sub=sub, lam_init=lam_init)


def _diff_attn(proj3d, lq1, lk1, lq2, lk2, subln_g, lam_init, tq, online):
    b, s, _ = proj3d.shape
    rows = tq if online else s
    heads = 1 if online else HEADS_PER_STEP
    groups = DIFF_HEADS // heads
    width = heads * DIFF_V_DIM
    vec = pl.BlockSpec((1, DIFF_HEAD_DIM), lambda bi, h, i: (0, 0))
    body = _diff_attn_online_kernel if online else _diff_attn_plain_kernel
    return pl.pallas_call(
        functools.partial(body, tq=tq, lam_init=lam_init),
        grid=(b, groups, s // rows),
        in_specs=[
            vec, vec, vec, vec,
            pl.BlockSpec((1, DIFF_V_DIM), lambda bi, h, i: (0, 0)),
            pl.BlockSpec((1, rows, width), lambda bi, h, i: (bi, i, COL_Q * groups + h)),
            pl.BlockSpec((1, s, width), lambda bi, h, i: (bi, 0, COL_K * groups + h)),
            pl.BlockSpec((1, s, width), lambda bi, h, i: (bi, 0, COL_V * groups + h)),
        ],
        out_specs=pl.BlockSpec((1, rows, width), lambda bi, h, i: (bi, i, h)),
        out_shape=jax.ShapeDtypeStruct((b, s, DIFF_HEADS * DIFF_V_DIM), BF16),
        compiler_params=pltpu.CompilerParams(
            dimension_semantics=("parallel", "parallel", "parallel"),
            vmem_limit_bytes=VMEM_LIMIT),
        name="diff_attn_online" if online else "diff_attn",
    )(lq1, lk1, lq2, lk2, subln_g, proj3d, proj3d, proj3d)


def _merge_kernel(x_ref, on_ref, yc_ref, om_ref, ga_ref, gc_ref, gm_ref,
                  wa_ref, wc_ref, wm_ref, wo_ref, o_ref):
    ya = _dot(on_ref[...], wa_ref[...])
    yc = _dot(yc_ref[...], wc_ref[...])
    ym = _dot(om_ref[...], wm_ref[...])

    merged = (ga_ref[...].astype(F32) * ya + gc_ref[...].astype(F32) * yc
              + gm_ref[...].astype(F32) * ym)
    o_ref[...] = x_ref[...] + _dot(merged.astype(BF16), wo_ref[...])


def _merge(x2d, on2d, proj, wa, wc, wm, wo, tm):
    m, d = x2d.shape

    def col(c):
        return pl.BlockSpec((tm, d), lambda i: (i, c))

    return pl.pallas_call(
        _merge_kernel,
        grid=(m // tm,),
        in_specs=[
            col(0), col(0),
            col(COL_YC), col(COL_OM), col(COL_GATE), col(COL_GATE + 1), col(COL_GATE + 2),
            _resident((d, d)), _resident((d, d)), _resident((d, d)), _resident((d, d)),
        ],
        out_specs=col(0),
        out_shape=jax.ShapeDtypeStruct((m, d), F32),
        compiler_params=pltpu.CompilerParams(
            dimension_semantics=("parallel",), vmem_limit_bytes=VMEM_LIMIT),
        name="merge",
    )(x2d, on2d, proj, proj, proj, proj, proj, wa, wc, wm, wo)


def _mlp_kernel(x_ref, g_ref, w1_ref, w2_ref, o_ref, *, tf):
    x = x_ref[...]
    h = _rms(x, g_ref[...]).astype(BF16)
    acc = x
    for f in range(w1_ref.shape[1] // tf):
        u = jnp.square(jnp.maximum(_dot(h, w1_ref[:, f * tf:(f + 1) * tf]), 0.0))
        acc = acc + _dot(u.astype(BF16), w2_ref[f * tf:(f + 1) * tf, :])
    o_ref[...] = acc


def _mlp(x2d, g, w1, w2, tm, tf):
    m, d = x2d.shape
    return pl.pallas_call(
        functools.partial(_mlp_kernel, tf=tf),
        grid=(m // tm,),
        in_specs=[
            pl.BlockSpec((tm, d), lambda i: (i, 0)),
            _resident((1, d)), _resident(w1.shape), _resident(w2.shape),
        ],
        out_specs=pl.BlockSpec((tm, d), lambda i: (i, 0)),
        out_shape=jax.ShapeDtypeStruct((m, d), F32),
        compiler_params=pltpu.CompilerParams(
            dimension_semantics=("parallel",), vmem_limit_bytes=VMEM_LIMIT),
        name="mlp",
    )(x2d, g, w1, w2)


def _layer(x, mem, lam_init, norm_mix_g, norm_mem_g, w_in, b_gate, q_norm_g, k_norm_g,
           lam_q1, lam_k1, lam_q2, lam_k2, subln_g, w_attn_o, conv_w, w_conv_o, w_mem_kv,
           mq_norm_g, mk_norm_g, w_mem_o, w_o, norm_mlp_g, w_mlp_in, w_mlp_out):
    b, s, d = x.shape
    x2d = x.reshape(b * s, d)
    row = lambda v: v.reshape(1, -1)

    gn = jnp.stack([
        jnp.tile(q_norm_g * (LOG2_E * DIFF_HEAD_DIM ** -0.5), d // DIFF_HEAD_DIM),
        jnp.tile(k_norm_g, d // DIFF_HEAD_DIM),
        jnp.tile(mq_norm_g * MEM_HEAD_DIM ** -0.5, d // MEM_HEAD_DIM),
    ]).reshape(3, 1, d)

    kt, vm, w_in16 = _mem_kv(mem, row(norm_mem_g), w_mem_kv, row(mk_norm_g), w_in)
    later = (w_attn_o, w_conv_o, w_mem_o, w_o, w_mlp_in, w_mlp_out)
    proj, (wa, wc, wm, wo, w1, w2) = _in_proj(
        x2d, row(norm_mix_g), w_in16, gn, conv_w, row(b_gate), kt, vm, later,
        tm=TM_IN_PROJ, seq=s)

    score_bound = DIFF_HEAD_DIM ** 0.5 * jnp.max(jnp.abs(q_norm_g)) * jnp.max(jnp.abs(k_norm_g))
    attn = functools.partial(_diff_attn, proj.reshape(b, s, -1), row(lam_q1), row(lam_k1),
                             row(lam_q2), row(lam_k2), row(subln_g), lam_init)
    o_n = lax.cond(score_bound <= MAX_UNSHIFTED_LOGIT,
                   lambda: attn(tq=TQ_PLAIN, online=False),
                   lambda: attn(tq=TQ_ONLINE, online=True))
    x1 = _merge(x2d, o_n.reshape(b * s, d), proj, wa, wc, wm, wo, tm=TM_MERGE)
    x2 = _mlp(x1, row(norm_mlp_g), w1, w2, tm=TM_MLP, tf=TF_MLP)
    return x2.reshape(b, s, d)


def kernel(x, mem, norm_mix_g, norm_mem_g, w_in, b_gate, q_norm_g, k_norm_g, lam_q1, lam_k1,
           lam_q2, lam_k2, subln_g, w_attn_o, conv_w, w_conv_o, w_mem_kv, mq_norm_g, mk_norm_g,
           w_mem_o, w_o, norm_mlp_g, w_mlp_in, w_mlp_out):
    depth = w_in.shape[0]
    for layer in range(depth):
        lam_init = 0.8 - 0.6 * float(np.exp(-0.3 * layer))
        x = _layer(x, mem, lam_init, norm_mix_g[layer], norm_mem_g[layer], w_in[layer],
                   b_gate[layer], q_norm_g[layer], k_norm_g[layer], lam_q1[layer], lam_k1[layer],
                   lam_q2[layer], lam_k2[layer], subln_g[layer], w_attn_o[layer], conv_w[layer],
                   w_conv_o[layer], w_mem_kv[layer], mq_norm_g[layer], mk_norm_g[layer],
                   w_mem_o[layer], w_o[layer], norm_mlp_g[layer], w_mlp_in[layer],
                   w_mlp_out[layer])
    return x
```

```python
import functools

import jax
import jax.numpy as jnp
import numpy as np
from jax import lax
from jax.experimental import pallas as pl
from jax.experimental.pallas import tpu as pltpu

F32 = jnp.float32
BF16 = jnp.bfloat16

NORM_EPS = 1e-6
MASK_VALUE = -1e30
LOG2_E = 1.4426950408889634
MAX_UNSHIFTED_LOGIT = 60.0

DIFF_HEADS = 8
DIFF_HEAD_DIM = 64
DIFF_V_DIM = 2 * DIFF_HEAD_DIM
MEM_HEADS = 4
MEM_HEAD_DIM = 256
CONV_K = 3

MXU_EDGE = 256
HALO = 8
BF16_SUBLANES = 16

W_Q, W_K, W_V, W_XC, W_GB, W_GC, W_QM, W_GATE = 0, 1, 2, 3, 4, 5, 6, 7
COL_Q, COL_K, COL_V, COL_YC, COL_OM, COL_GATE = 0, 1, 2, 3, 4, 5
N_COL_BLOCKS = 8

VMEM_LIMIT = 56 * 1024 * 1024

TM_IN_PROJ = 512
TM_MERGE = 1024
TM_MLP = 1024
TF_MLP = 1024
TQ_PLAIN = 1024
TQ_ONLINE = 512


def _rms(x, g):
    ms = jnp.mean(x * x, axis=-1, keepdims=True)
    return x * lax.rsqrt(ms + NORM_EPS) * g


def _dot(a, b):
    return jnp.dot(a, b, preferred_element_type=F32)


def _dot_nt(a, b):
    return lax.dot_general(a, b, (((1,), (1,)), ((), ())), preferred_element_type=F32)


def _resident(shape):
    return pl.BlockSpec(shape, lambda i: (0,) * len(shape), pipeline_mode=pl.Buffered(1))


def _mem_kv_kernel(mem_ref, g_ref, w_ref, kg_ref, win_ref, kt_ref, v_ref, win16_ref):
    win16_ref[...] = win_ref[...].astype(BF16)
    mem_n = _rms(mem_ref[0], g_ref[...]).astype(BF16)
    kv = _dot(mem_n, w_ref[...].astype(BF16))
    width = kt_ref.shape[1]
    for h in range(MEM_HEADS):
        sl = slice(h * MEM_HEAD_DIM, (h + 1) * MEM_HEAD_DIM)
        kn = _rms(kv[:, sl], kg_ref[...])
        kt_ref[0, sl, :] = kn.T.astype(BF16)
    v_ref[0] = kv[:, width:].astype(BF16)


def _mem_kv(mem, g, w_kv, kg, w_in):
    b, m, d = mem.shape
    width = w_kv.shape[1] // 2
    assert width == MEM_HEADS * MEM_HEAD_DIM and w_in.shape[0] % (BF16_SUBLANES * b) == 0
    w_in_chunk = pl.BlockSpec((w_in.shape[0] // b, w_in.shape[1]), lambda i: (i, 0))
    return pl.pallas_call(
        _mem_kv_kernel,
        grid=(b,),
        in_specs=[
            pl.BlockSpec((1, m, d), lambda i: (i, 0, 0)),
            _resident((1, d)), _resident(w_kv.shape), _resident((1, MEM_HEAD_DIM)),
            w_in_chunk,
        ],
        out_specs=[
            pl.BlockSpec((1, width, m), lambda i: (i, 0, 0)),
            pl.BlockSpec((1, m, width), lambda i: (i, 0, 0)),
            w_in_chunk,
        ],
        out_shape=[
            jax.ShapeDtypeStruct((b, width, m), BF16),
            jax.ShapeDtypeStruct((b, m, width), BF16),
            jax.ShapeDtypeStruct(w_in.shape, BF16),
        ],
        compiler_params=pltpu.CompilerParams(
            dimension_semantics=("parallel",), vmem_limit_bytes=VMEM_LIMIT),
        name="mem_kv",
    )(mem, g, w_kv, kg, w_in)


def _in_proj_kernel(x_ref, g_ref, w_ref, gn_ref, cw_ref, bg_ref, kt_ref, vm_ref, *refs,
                    tiles_per_seq):
    n_cast = (len(refs) - 2) // 2
    o_ref, inner_scr = refs[n_cast], refs[-1]
    for w32_ref, w16_ref in zip(refs[:n_cast], refs[n_cast + 1:-1]):
        w16_ref[...] = w32_ref[...].astype(BF16)
    i = pl.program_id(0)
    tm, d = x_ref.shape

    @pl.when(i % tiles_per_seq == 0)
    def _():
        inner_scr[0:HALO, :] = jnp.zeros((HALO, d), F32)

    h = _rms(x_ref[...], g_ref[...]).astype(BF16)

    def proj(c):
        return _dot(h, w_ref[:, c * d:(c + 1) * d])

    def put(c, val):
        o_ref[:, c * d:(c + 1) * d] = val.astype(BF16)

    def gates():
        for t in range(3):
            z = proj(W_GATE + t) + bg_ref[:, t * d:(t + 1) * d]
            put(COL_GATE + t, 0.5 + 0.5 * jnp.tanh(0.5 * z))

    def conv():
        inner = proj(W_GC) * proj(W_XC)
        inner_scr[HALO:HALO + tm, :] = inner
        taps = (cw_ref[2:3, :] * inner
                + cw_ref[1:2, :] * inner_scr[HALO - 1:HALO - 1 + tm, :]
                + cw_ref[0:1, :] * inner_scr[HALO - 2:HALO - 2 + tm, :])
        put(COL_YC, proj(W_GB) * taps)
        inner_scr[0:HALO, :] = inner_scr[tm:tm + HALO, :]

    def group_normed(slot, src, dst):
        acc = proj(src)
        lane = lax.broadcasted_iota(jnp.int32, (tm, 2 * DIFF_HEAD_DIM), 1)
        low = lane < DIFF_HEAD_DIM
        for c in range(d // (2 * DIFF_HEAD_DIM)):
            sl = slice(c * 2 * DIFF_HEAD_DIM, (c + 1) * 2 * DIFF_HEAD_DIM)
            a = acc[:, sl]
            sq = a * a
            s_lo = jnp.sum(jnp.where(low, sq, 0.0), axis=-1, keepdims=True)
            s_hi = jnp.sum(jnp.where(low, 0.0, sq), axis=-1, keepdims=True)
            ms = jnp.where(low, s_lo, s_hi) * (1.0 / DIFF_HEAD_DIM)
            y = a * lax.rsqrt(ms + NORM_EPS) * gn_ref[slot, :, sl]
            o_ref[:, dst * d + sl.start:dst * d + sl.stop] = y.astype(BF16)

    def memory_attention():
        acc = proj(W_QM)
        for hd in range(MEM_HEADS):
            sl = slice(hd * MEM_HEAD_DIM, (hd + 1) * MEM_HEAD_DIM)
            qn = _rms(acc[:, sl], gn_ref[2, :, sl]).astype(BF16)
            s = _dot(qn, kt_ref[0, sl, :])
            p = jnp.exp(s - jnp.max(s, axis=-1, keepdims=True))
            p = p / jnp.sum(p, axis=-1, keepdims=True)
            o_ref[:, COL_OM * d + sl.start:COL_OM * d + sl.stop] = _dot(
                p.astype(BF16), vm_ref[0, :, sl]).astype(BF16)

    memory_attention()
    group_normed(1, W_K, COL_K)
    group_normed(0, W_Q, COL_Q)
    gates()
    conv()
    put(COL_V, proj(W_V))


def _in_proj(x2d, g, w_in, gn, conv_w, b_gate, kt, vm, later_weights, tm, seq):
    m, d = x2d.shape
    assert seq % tm == 0 and m % seq == 0, (m, seq, tm)
    assert conv_w.shape == (CONV_K, d) and CONV_K - 1 <= HALO, conv_w.shape
    assert w_in.shape == (d, (W_GATE + 3) * d), w_in.shape
    tiles = m // tm
    tiles_per_seq = seq // tm
    assert all(w.shape[0] % (BF16_SUBLANES * tiles) == 0 for w in later_weights)
    chunks = [pl.BlockSpec((w.shape[0] // tiles, w.shape[1]), lambda i: (i, 0))
              for w in later_weights]
    outs = pl.pallas_call(
        functools.partial(_in_proj_kernel, tiles_per_seq=tiles_per_seq),
        grid=(tiles,),
        in_specs=[
            pl.BlockSpec((tm, d), lambda i: (i, 0)),
            _resident((1, d)), _resident(w_in.shape),
            _resident(gn.shape), _resident(conv_w.shape), _resident(b_gate.shape),
            pl.BlockSpec((1,) + kt.shape[1:], lambda i: (i // tiles_per_seq, 0, 0)),
            pl.BlockSpec((1,) + vm.shape[1:], lambda i: (i // tiles_per_seq, 0, 0)),
        ] + chunks,
        out_specs=[pl.BlockSpec((tm, N_COL_BLOCKS * d), lambda i: (i, 0))] + chunks,
        out_shape=[jax.ShapeDtypeStruct((m, N_COL_BLOCKS * d), BF16)]
        + [jax.ShapeDtypeStruct(w.shape, BF16) for w in later_weights],
        scratch_shapes=[pltpu.VMEM((HALO + tm, d), F32)],
        compiler_params=pltpu.CompilerParams(
            dimension_semantics=("arbitrary",), vmem_limit_bytes=VMEM_LIMIT),
        name="in_proj",
    )(x2d, g, w_in, gn, conv_w, b_gate, kt, vm, *later_weights)
    return outs[0], tuple(outs[1:])


def _stack_maps(q):
    lane = lax.broadcasted_iota(jnp.int32, q.shape, 1)
    zero = jnp.zeros_like(q)
    return jnp.concatenate([jnp.where(lane < DIFF_HEAD_DIM, q, zero),
                            jnp.where(lane >= DIFF_HEAD_DIM, q, zero)], axis=0)


def _causal_mask(s, n):
    row = lax.broadcasted_iota(jnp.int32, (n, n), 0)
    col = lax.broadcasted_iota(jnp.int32, (n, n), 1)
    keep = jnp.concatenate([col <= row] * (s.shape[0] // n), axis=0)
    return jnp.where(keep, s, MASK_VALUE)


def _diff_lambda(lq1_ref, lk1_ref, lq2_ref, lk2_ref, lam_init):
    return (jnp.exp(jnp.sum(lq1_ref[...] * lk1_ref[...], axis=-1, keepdims=True))
            - jnp.exp(jnp.sum(lq2_ref[...] * lk2_ref[...], axis=-1, keepdims=True))
            + lam_init)


def _diff_finalize(acc, l, lam, sg, lam_init):
    n = acc.shape[0] // 2
    o = acc[:n] / l[:n] - lam * (acc[n:] / l[n:])
    return (_rms(o, sg) * (1.0 - lam_init)).astype(BF16)


def _diff_attn_online_kernel(lq1_ref, lk1_ref, lq2_ref, lk2_ref, sg_ref, q_ref, k_ref, v_ref,
                             o_ref, *, tq, lam_init):
    i = pl.program_id(2)
    qs = _stack_maps(q_ref[0])

    def step(j, carry, masked):
        m, l, acc = carry
        start = pl.multiple_of(j * tq, tq)
        s = _dot_nt(qs, k_ref[0, pl.ds(start, tq), :])
        if masked:
            s = _causal_mask(s, tq)
        m_new = jnp.maximum(m, jnp.max(s, axis=-1, keepdims=True))
        alpha = jnp.exp2(m - m_new)
        p = jnp.exp2(s - m_new)
        l = alpha * l + jnp.sum(p, axis=-1, keepdims=True)
        acc = alpha * acc + _dot(p.astype(BF16), v_ref[0, pl.ds(start, tq), :])
        return m_new, l, acc

    init = (jnp.full((2 * tq, 1), MASK_VALUE, F32), jnp.zeros((2 * tq, 1), F32),
            jnp.zeros((2 * tq, DIFF_V_DIM), F32))
    carry = lax.fori_loop(0, i, functools.partial(step, masked=False), init)
    _, l, acc = step(i, carry, masked=True)
    lam = _diff_lambda(lq1_ref, lk1_ref, lq2_ref, lk2_ref, lam_init)
    o_ref[0] = _diff_finalize(acc, l, lam, sg_ref[...], lam_init)


def _diff_attn_plain_kernel(lq1_ref, lk1_ref, lq2_ref, lk2_ref, sg_ref, q_ref, k_ref, v_ref,
                            o_ref, *, tq, lam_init, sub=MXU_EDGE):
    seq = q_ref.shape[1]
    lam = _diff_lambda(lq1_ref, lk1_ref, lq2_ref, lk2_ref, lam_init)

    def pv(s, start, size):
        v = v_ref[0, start:start + size, :]
        v_ones = jnp.concatenate([v, jnp.ones_like(v)], axis=1)
        return _dot(jnp.exp2(s).astype(BF16), v_ones)

    for i in reversed(range(seq // tq)):
        r0 = i * tq
        qs = _stack_maps(q_ref[0, r0:r0 + tq, :])
        acc = None
        for c in range(tq // sub):
            n = tq - c * sub
            qs_c = qs if c == 0 else jnp.concatenate([qs[c * sub:tq], qs[tq + c * sub:]], axis=0)
            s = _dot_nt(qs_c, k_ref[0, r0 + c * sub:r0 + (c + 1) * sub, :])
            pieces = [_causal_mask(s[:sub], sub), s[sub:n],
                      _causal_mask(s[n:n + sub], sub), s[n + sub:]]
            s = jnp.concatenate([p for p in pieces if p.shape[0]], axis=0)
            part = pv(s, r0 + c * sub, sub)
            if c == 0:
                acc = part
            else:
                acc = jnp.concatenate([acc[:c * sub], acc[c * sub:tq] + part[:n],
                                       acc[tq:tq + c * sub], acc[tq + c * sub:] + part[n:]],
                                      axis=0)
        for j in range(i):
            acc = acc + pv(_dot_nt(qs, k_ref[0, j * tq:(j + 1) * tq, :]), j * tq, tq)
        o_ref[0, r0:r0 + tq, :] = _diff_finalize(acc[:, :DIFF_V_DIM], acc[:, DIFF_V_DIM:], lam,
                                                 sg_ref[...], lam_init)


def _diff_attn(proj3d, lq1, lk1, lq2, lk2, subln_g, lam_init, tq, online):
    b, s, _ = proj3d.shape
    assert s % tq == 0 and tq % MXU_EDGE == 0, (s, tq)
    rows = tq if online else s
    grid = (b, DIFF_HEADS, s // rows)
    vec = pl.BlockSpec((1, DIFF_HEAD_DIM), lambda bi, h, i: (0, 0))
    body = _diff_attn_online_kernel if online else _diff_attn_plain_kernel
    return pl.pallas_call(
        functools.partial(body, tq=tq, lam_init=lam_init),
        grid=grid,
        in_specs=[
            vec, vec, vec, vec,
            pl.BlockSpec((1, DIFF_V_DIM), lambda bi, h, i: (0, 0)),
            pl.BlockSpec((1, rows, DIFF_V_DIM), lambda bi, h, i: (bi, i, COL_Q * DIFF_HEADS + h)),
            pl.BlockSpec((1, s, DIFF_V_DIM), lambda bi, h, i: (bi, 0, COL_K * DIFF_HEADS + h)),
            pl.BlockSpec((1, s, DIFF_V_DIM), lambda bi, h, i: (bi, 0, COL_V * DIFF_HEADS + h)),
        ],
        out_specs=pl.BlockSpec((1, rows, DIFF_V_DIM), lambda bi, h, i: (bi, i, h)),
        out_shape=jax.ShapeDtypeStruct((b, s, DIFF_HEADS * DIFF_V_DIM), BF16),
        compiler_params=pltpu.CompilerParams(
            dimension_semantics=("parallel", "parallel", "parallel"),
            vmem_limit_bytes=VMEM_LIMIT),
        name="diff_attn_online" if online else "diff_attn",
    )(lq1, lk1, lq2, lk2, subln_g, proj3d, proj3d, proj3d)


def _merge_kernel(x_ref, on_ref, yc_ref, om_ref, ga_ref, gc_ref, gm_ref,
                  wa_ref, wc_ref, wm_ref, wo_ref, o_ref):
    ya = _dot(on_ref[...], wa_ref[...])
    yc = _dot(yc_ref[...], wc_ref[...])
    ym = _dot(om_ref[...], wm_ref[...])

    merged = (ga_ref[...].astype(F32) * ya + gc_ref[...].astype(F32) * yc
              + gm_ref[...].astype(F32) * ym)
    o_ref[...] = x_ref[...] + _dot(merged.astype(BF16), wo_ref[...])


def _merge(x2d, on2d, proj, wa, wc, wm, wo, tm):
    m, d = x2d.shape
    assert m % tm == 0 and proj.shape == (m, N_COL_BLOCKS * d), (m, tm, proj.shape)

    def col(c):
        return pl.BlockSpec((tm, d), lambda i: (i, c))

    return pl.pallas_call(
        _merge_kernel,
        grid=(m // tm,),
        in_specs=[
            col(0), col(0),
            col(COL_YC), col(COL_OM), col(COL_GATE), col(COL_GATE + 1), col(COL_GATE + 2),
            _resident((d, d)), _resident((d, d)), _resident((d, d)), _resident((d, d)),
        ],
        out_specs=col(0),
        out_shape=jax.ShapeDtypeStruct((m, d), F32),
        compiler_params=pltpu.CompilerParams(
            dimension_semantics=("parallel",), vmem_limit_bytes=VMEM_LIMIT),
        name="merge",
    )(x2d, on2d, proj, proj, proj, proj, proj, wa, wc, wm, wo)


def _mlp_kernel(x_ref, g_ref, w1_ref, w2_ref, o_ref, *, tf):
    x = x_ref[...]
    h = _rms(x, g_ref[...]).astype(BF16)
    acc = x
    for f in range(w1_ref.shape[1] // tf):
        u = jnp.square(jnp.maximum(_dot(h, w1_ref[:, f * tf:(f + 1) * tf]), 0.0))
        acc = acc + _dot(u.astype(BF16), w2_ref[f * tf:(f + 1) * tf, :])
    o_ref[...] = acc


def _mlp(x2d, g, w1, w2, tm, tf):
    m, d = x2d.shape
    assert m % tm == 0 and w1.shape[1] % tf == 0 and w2.shape == w1.shape[::-1], (m, tm, tf)
    return pl.pallas_call(
        functools.partial(_mlp_kernel, tf=tf),
        grid=(m // tm,),
        in_specs=[
            pl.BlockSpec((tm, d), lambda i: (i, 0)),
            _resident((1, d)), _resident(w1.shape), _resident(w2.shape),
        ],
        out_specs=pl.BlockSpec((tm, d), lambda i: (i, 0)),
        out_shape=jax.ShapeDtypeStruct((m, d), F32),
        compiler_params=pltpu.CompilerParams(
            dimension_semantics=("parallel",), vmem_limit_bytes=VMEM_LIMIT),
        name="mlp",
    )(x2d, g, w1, w2)


def _layer(x, mem, lam_init, norm_mix_g, norm_mem_g, w_in, b_gate, q_norm_g, k_norm_g,
           lam_q1, lam_k1, lam_q2, lam_k2, subln_g, w_attn_o, conv_w, w_conv_o, w_mem_kv,
           mq_norm_g, mk_norm_g, w_mem_o, w_o, norm_mlp_g, w_mlp_in, w_mlp_out):
    b, s, d = x.shape
    x2d = x.reshape(b * s, d)
    row = lambda v: v.reshape(1, -1)

    gn = jnp.stack([
        jnp.tile(q_norm_g * (LOG2_E * DIFF_HEAD_DIM ** -0.5), d // DIFF_HEAD_DIM),
        jnp.tile(k_norm_g, d // DIFF_HEAD_DIM),
        jnp.tile(mq_norm_g * MEM_HEAD_DIM ** -0.5, d // MEM_HEAD_DIM),
    ]).reshape(3, 1, d)

    kt, vm, w_in16 = _mem_kv(mem, row(norm_mem_g), w_mem_kv, row(mk_norm_g), w_in)
    later = (w_attn_o, w_conv_o, w_mem_o, w_o, w_mlp_in, w_mlp_out)
    proj, (wa, wc, wm, wo, w1, w2) = _in_proj(
        x2d, row(norm_mix_g), w_in16, gn, conv_w, row(b_gate), kt, vm, later,
        tm=TM_IN_PROJ, seq=s)

    score_bound = DIFF_HEAD_DIM ** 0.5 * jnp.max(jnp.abs(q_norm_g)) * jnp.max(jnp.abs(k_norm_g))
    attn = functools.partial(_diff_attn, proj.reshape(b, s, -1), row(lam_q1), row(lam_k1),
                             row(lam_q2), row(lam_k2), row(subln_g), lam_init)
    o_n = lax.cond(score_bound <= MAX_UNSHIFTED_LOGIT,
                   lambda: attn(tq=TQ_PLAIN, online=False),
                   lambda: attn(tq=TQ_ONLINE, online=True))
    x1 = _merge(x2d, o_n.reshape(b * s, d), proj, wa, wc, wm, wo, tm=TM_MERGE)
    x2 = _mlp(x1, row(norm_mlp_g), w1, w2, tm=TM_MLP, tf=TF_MLP)
    return x2.reshape(b, s, d)


def kernel(x, mem, norm_mix_g, norm_mem_g, w_in, b_gate, q_norm_g, k_norm_g, lam_q1, lam_k1,
           lam_q2, lam_k2, subln_g, w_attn_o, conv_w, w_conv_o, w_mem_kv, mq_norm_g, mk_norm_g,
           w_mem_o, w_o, norm_mlp_g, w_mlp_in, w_mlp_out):
    depth = w_in.shape[0]
    for layer in range(depth):
        lam_init = 0.8 - 0.6 * float(np.exp(-0.3 * layer))
        x = _layer(x, mem, lam_init, norm_mix_g[layer], norm_mem_g[layer], w_in[layer],
                   b_gate[layer], q_norm_g[layer], k_norm_g[layer], lam_q1[layer], lam_k1[layer],
                   lam_q2[layer], lam_k2[layer], subln_g[layer], w_attn_o[layer], conv_w[layer],
                   w_conv_o[layer], w_mem_kv[layer], mq_norm_g[layer], mk_norm_g[layer],
                   w_mem_o[layer], w_o[layer], norm_mlp_g[layer], w_mlp_in[layer],
                   w_mlp_out[layer])
    return x
```

```python
import functools

import jax
import jax.numpy as jnp
import numpy as np
from jax import lax
from jax.experimental import pallas as pl
from jax.experimental.pallas import tpu as pltpu

F32 = jnp.float32
BF16 = jnp.bfloat16

NORM_EPS = 1e-6
MASK_VALUE = -1e30
LOG2_E = 1.4426950408889634
MAX_UNSHIFTED_LOGIT = 60.0

DIFF_HEADS = 8
DIFF_HEAD_DIM = 64
DIFF_V_DIM = 2 * DIFF_HEAD_DIM
MEM_HEADS = 4
MEM_HEAD_DIM = 256
CONV_K = 3

MXU_EDGE = 256
HALO = 8
BF16_SUBLANES = 16

W_Q, W_K, W_V, W_XC, W_GB, W_GC, W_QM, W_GATE = 0, 1, 2, 3, 4, 5, 6, 7
COL_Q, COL_K, COL_V, COL_YC, COL_OM, COL_GATE = 0, 1, 2, 3, 4, 5
N_COL_BLOCKS = 8

VMEM_LIMIT = 56 * 1024 * 1024

TM_IN_PROJ = 512
TM_MERGE = 1024
TM_MLP = 1024
TF_MLP = 1024
TQ_PLAIN = 512
TQ_ONLINE = 512


def _rms(x, g):
    ms = jnp.mean(x * x, axis=-1, keepdims=True)
    return x * lax.rsqrt(ms + NORM_EPS) * g


def _dot(a, b):
    return jnp.dot(a, b, preferred_element_type=F32)


def _dot_nt(a, b):
    return lax.dot_general(a, b, (((1,), (1,)), ((), ())), preferred_element_type=F32)


def _resident(shape):
    return pl.BlockSpec(shape, lambda i: (0,) * len(shape), pipeline_mode=pl.Buffered(1))


def _mem_kv_kernel(mem_ref, g_ref, w_ref, kg_ref, win_ref, kt_ref, v_ref, win16_ref):
    win16_ref[...] = win_ref[...].astype(BF16)
    mem_n = _rms(mem_ref[0], g_ref[...]).astype(BF16)
    kv = _dot(mem_n, w_ref[...].astype(BF16))
    width = kt_ref.shape[1]
    for h in range(MEM_HEADS):
        sl = slice(h * MEM_HEAD_DIM, (h + 1) * MEM_HEAD_DIM)
        kn = _rms(kv[:, sl], kg_ref[...])
        kt_ref[0, sl, :] = kn.T.astype(BF16)
    v_ref[0] = kv[:, width:].astype(BF16)


def _mem_kv(mem, g, w_kv, kg, w_in):
    b, m, d = mem.shape
    width = w_kv.shape[1] // 2
    assert width == MEM_HEADS * MEM_HEAD_DIM and w_in.shape[0] % (BF16_SUBLANES * b) == 0
    w_in_chunk = pl.BlockSpec((w_in.shape[0] // b, w_in.shape[1]), lambda i: (i, 0))
    return pl.pallas_call(
        _mem_kv_kernel,
        grid=(b,),
        in_specs=[
            pl.BlockSpec((1, m, d), lambda i: (i, 0, 0)),
            _resident((1, d)), _resident(w_kv.shape), _resident((1, MEM_HEAD_DIM)),
            w_in_chunk,
        ],
        out_specs=[
            pl.BlockSpec((1, width, m), lambda i: (i, 0, 0)),
            pl.BlockSpec((1, m, width), lambda i: (i, 0, 0)),
            w_in_chunk,
        ],
        out_shape=[
            jax.ShapeDtypeStruct((b, width, m), BF16),
            jax.ShapeDtypeStruct((b, m, width), BF16),
            jax.ShapeDtypeStruct(w_in.shape, BF16),
        ],
        compiler_params=pltpu.CompilerParams(
            dimension_semantics=("parallel",), vmem_limit_bytes=VMEM_LIMIT),
        name="mem_kv",
    )(mem, g, w_kv, kg, w_in)


def _in_proj_kernel(x_ref, g_ref, w_ref, gn_ref, cw_ref, bg_ref, kt_ref, vm_ref, *refs,
                    tiles_per_seq):
    n_cast = (len(refs) - 2) // 2
    o_ref, inner_scr = refs[n_cast], refs[-1]
    for w32_ref, w16_ref in zip(refs[:n_cast], refs[n_cast + 1:-1]):
        w16_ref[...] = w32_ref[...].astype(BF16)
    i = pl.program_id(0)
    tm, d = x_ref.shape

    @pl.when(i % tiles_per_seq == 0)
    def _():
        inner_scr[0:HALO, :] = jnp.zeros((HALO, d), F32)

    h = _rms(x_ref[...], g_ref[...]).astype(BF16)

    def proj(c):
        return _dot(h, w_ref[:, c * d:(c + 1) * d])

    def put(c, val):
        o_ref[:, c * d:(c + 1) * d] = val.astype(BF16)

    def gates():
        for t in range(3):
            z = proj(W_GATE + t) + bg_ref[:, t * d:(t + 1) * d]
            put(COL_GATE + t, 0.5 + 0.5 * jnp.tanh(0.5 * z))

    def conv():
        inner = proj(W_GC) * proj(W_XC)
        inner_scr[HALO:HALO + tm, :] = inner
        taps = (cw_ref[2:3, :] * inner
                + cw_ref[1:2, :] * inner_scr[HALO - 1:HALO - 1 + tm, :]
                + cw_ref[0:1, :] * inner_scr[HALO - 2:HALO - 2 + tm, :])
        put(COL_YC, proj(W_GB) * taps)
        inner_scr[0:HALO, :] = inner_scr[tm:tm + HALO, :]

    def group_normed(slot, src, dst):
        acc = proj(src)
        lane = lax.broadcasted_iota(jnp.int32, (tm, 2 * DIFF_HEAD_DIM), 1)
        low = lane < DIFF_HEAD_DIM
        for c in range(d // (2 * DIFF_HEAD_DIM)):
            sl = slice(c * 2 * DIFF_HEAD_DIM, (c + 1) * 2 * DIFF_HEAD_DIM)
            a = acc[:, sl]
            sq = a * a
            s_lo = jnp.sum(jnp.where(low, sq, 0.0), axis=-1, keepdims=True)
            s_hi = jnp.sum(jnp.where(low, 0.0, sq), axis=-1, keepdims=True)
            ms = jnp.where(low, s_lo, s_hi) * (1.0 / DIFF_HEAD_DIM)
            y = a * lax.rsqrt(ms + NORM_EPS) * gn_ref[slot, :, sl]
            o_ref[:, dst * d + sl.start:dst * d + sl.stop] = y.astype(BF16)

    def memory_attention():
        acc = proj(W_QM)
        for hd in range(MEM_HEADS):
            sl = slice(hd * MEM_HEAD_DIM, (hd + 1) * MEM_HEAD_DIM)
            qn = _rms(acc[:, sl], gn_ref[2, :, sl]).astype(BF16)
            s = _dot(qn, kt_ref[0, sl, :])
            p = jnp.exp(s - jnp.max(s, axis=-1, keepdims=True))
            p = p / jnp.sum(p, axis=-1, keepdims=True)
            o_ref[:, COL_OM * d + sl.start:COL_OM * d + sl.stop] = _dot(
                p.astype(BF16), vm_ref[0, :, sl]).astype(BF16)

    memory_attention()
    group_normed(1, W_K, COL_K)
    group_normed(0, W_Q, COL_Q)
    gates()
    conv()
    put(COL_V, proj(W_V))


def _in_proj(x2d, g, w_in, gn, conv_w, b_gate, kt, vm, later_weights, tm, seq):
    m, d = x2d.shape
    assert seq % tm == 0 and m % seq == 0, (m, seq, tm)
    assert conv_w.shape == (CONV_K, d) and CONV_K - 1 <= HALO, conv_w.shape
    assert w_in.shape == (d, (W_GATE + 3) * d), w_in.shape
    tiles = m // tm
    tiles_per_seq = seq // tm
    assert all(w.shape[0] % (BF16_SUBLANES * tiles) == 0 for w in later_weights)
    chunks = [pl.BlockSpec((w.shape[0] // tiles, w.shape[1]), lambda i: (i, 0))
              for w in later_weights]
    outs = pl.pallas_call(
        functools.partial(_in_proj_kernel, tiles_per_seq=tiles_per_seq),
        grid=(tiles,),
        in_specs=[
            pl.BlockSpec((tm, d), lambda i: (i, 0)),
            _resident((1, d)), _resident(w_in.shape),
            _resident(gn.shape), _resident(conv_w.shape), _resident(b_gate.shape),
            pl.BlockSpec((1,) + kt.shape[1:], lambda i: (i // tiles_per_seq, 0, 0)),
            pl.BlockSpec((1,) + vm.shape[1:], lambda i: (i // tiles_per_seq, 0, 0)),
        ] + chunks,
        out_specs=[pl.BlockSpec((tm, N_COL_BLOCKS * d), lambda i: (i, 0))] + chunks,
        out_shape=[jax.ShapeDtypeStruct((m, N_COL_BLOCKS * d), BF16)]
        + [jax.ShapeDtypeStruct(w.shape, BF16) for w in later_weights],
        scratch_shapes=[pltpu.VMEM((HALO + tm, d), F32)],
        compiler_params=pltpu.CompilerParams(
            dimension_semantics=("arbitrary",), vmem_limit_bytes=VMEM_LIMIT),
        name="in_proj",
    )(x2d, g, w_in, gn, conv_w, b_gate, kt, vm, *later_weights)
    return outs[0], tuple(outs[1:])


def _stack_maps(q):
    lane = lax.broadcasted_iota(jnp.int32, q.shape, 1)
    zero = jnp.zeros_like(q)
    return jnp.concatenate([jnp.where(lane < DIFF_HEAD_DIM, q, zero),
                            jnp.where(lane >= DIFF_HEAD_DIM, q, zero)], axis=0)


def _causal_mask(s, n):
    row = lax.broadcasted_iota(jnp.int32, (n, n), 0)
    col = lax.broadcasted_iota(jnp.int32, (n, n), 1)
    keep = jnp.concatenate([col <= row] * (s.shape[0] // n), axis=0)
    return jnp.where(keep, s, MASK_VALUE)


def _diff_lambda(lq1_ref, lk1_ref, lq2_ref, lk2_ref, lam_init):
    return (jnp.exp(jnp.sum(lq1_ref[...] * lk1_ref[...], axis=-1, keepdims=True))
            - jnp.exp(jnp.sum(lq2_ref[...] * lk2_ref[...], axis=-1, keepdims=True))
            + lam_init)


def _diff_finalize(acc, l, lam, sg, lam_init):
    n = acc.shape[0] // 2
    o = acc[:n] / l[:n] - lam * (acc[n:] / l[n:])
    return (_rms(o, sg) * (1.0 - lam_init)).astype(BF16)


def _diff_attn_online_kernel(lq1_ref, lk1_ref, lq2_ref, lk2_ref, sg_ref, q_ref, k_ref, v_ref,
                             o_ref, *, tq, lam_init):
    i = pl.program_id(2)
    qs = _stack_maps(q_ref[0])

    def step(j, carry, masked):
        m, l, acc = carry
        start = pl.multiple_of(j * tq, tq)
        s = _dot_nt(qs, k_ref[0, pl.ds(start, tq), :])
        if masked:
            s = _causal_mask(s, tq)
        m_new = jnp.maximum(m, jnp.max(s, axis=-1, keepdims=True))
        alpha = jnp.exp2(m - m_new)
        p = jnp.exp2(s - m_new)
        l = alpha * l + jnp.sum(p, axis=-1, keepdims=True)
        acc = alpha * acc + _dot(p.astype(BF16), v_ref[0, pl.ds(start, tq), :])
        return m_new, l, acc

    init = (jnp.full((2 * tq, 1), MASK_VALUE, F32), jnp.zeros((2 * tq, 1), F32),
            jnp.zeros((2 * tq, DIFF_V_DIM), F32))
    carry = lax.fori_loop(0, i, functools.partial(step, masked=False), init)
    _, l, acc = step(i, carry, masked=True)
    lam = _diff_lambda(lq1_ref, lk1_ref, lq2_ref, lk2_ref, lam_init)
    o_ref[0] = _diff_finalize(acc, l, lam, sg_ref[...], lam_init)


def _diff_attn_plain_kernel(lq1_ref, lk1_ref, lq2_ref, lk2_ref, sg_ref, q_ref, k_ref, v_ref,
                            o_ref, *, tq, lam_init, sub=MXU_EDGE):
    seq = q_ref.shape[1]
    lam = _diff_lambda(lq1_ref, lk1_ref, lq2_ref, lk2_ref, lam_init)

    def pv(s, start, size):
        v = v_ref[0, start:start + size, :]
        v_ones = jnp.concatenate([v, jnp.ones_like(v)], axis=1)
        return _dot(jnp.exp2(s).astype(BF16), v_ones)

    for i in reversed(range(seq // tq)):
        r0 = i * tq
        qs = _stack_maps(q_ref[0, r0:r0 + tq, :])
        acc = None
        for c in range(tq // sub):
            n = tq - c * sub
            qs_c = qs if c == 0 else jnp.concatenate([qs[c * sub:tq], qs[tq + c * sub:]], axis=0)
            s = _dot_nt(qs_c, k_ref[0, r0 + c * sub:r0 + (c + 1) * sub, :])
            pieces = [_causal_mask(s[:sub], sub), s[sub:n],
                      _causal_mask(s[n:n + sub], sub), s[n + sub:]]
            s = jnp.concatenate([p for p in pieces if p.shape[0]], axis=0)
            part = pv(s, r0 + c * sub, sub)
            if c == 0:
                acc = part
            else:
                acc = jnp.concatenate([acc[:c * sub], acc[c * sub:tq] + part[:n],
                                       acc[tq:tq + c * sub], acc[tq + c * sub:] + part[n:]],
                                      axis=0)
        for j in range(i):
            acc = acc + pv(_dot_nt(qs, k_ref[0, j * tq:(j + 1) * tq, :]), j * tq, tq)
        o_ref[0, r0:r0 + tq, :] = _diff_finalize(acc[:, :DIFF_V_DIM], acc[:, DIFF_V_DIM:], lam,
                                                 sg_ref[...], lam_init)


def _diff_attn(proj3d, lq1, lk1, lq2, lk2, subln_g, lam_init, tq, online):
    b, s, _ = proj3d.shape
    assert s % tq == 0 and tq % MXU_EDGE == 0, (s, tq)
    rows = tq if online else s
    grid = (b, DIFF_HEADS, s // rows)
    vec = pl.BlockSpec((1, DIFF_HEAD_DIM), lambda bi, h, i: (0, 0))
    body = _diff_attn_online_kernel if online else _diff_attn_plain_kernel
    return pl.pallas_call(
        functools.partial(body, tq=tq, lam_init=lam_init),
        grid=grid,
        in_specs=[
            vec, vec, vec, vec,
            pl.BlockSpec((1, DIFF_V_DIM), lambda bi, h, i: (0, 0)),
            pl.BlockSpec((1, rows, DIFF_V_DIM), lambda bi, h, i: (bi, i, COL_Q * DIFF_HEADS + h)),
            pl.BlockSpec((1, s, DIFF_V_DIM), lambda bi, h, i: (bi, 0, COL_K * DIFF_HEADS + h)),
            pl.BlockSpec((1, s, DIFF_V_DIM), lambda bi, h, i: (bi, 0, COL_V * DIFF_HEADS + h)),
        ],
        out_specs=pl.BlockSpec((1, rows, DIFF_V_DIM), lambda bi, h, i: (bi, i, h)),
        out_shape=jax.ShapeDtypeStruct((b, s, DIFF_HEADS * DIFF_V_DIM), BF16),
        compiler_params=pltpu.CompilerParams(
            dimension_semantics=("parallel", "parallel", "parallel"),
            vmem_limit_bytes=VMEM_LIMIT),
        name="diff_attn_online" if online else "diff_attn",
    )(lq1, lk1, lq2, lk2, subln_g, proj3d, proj3d, proj3d)


def _merge_kernel(x_ref, on_ref, yc_ref, om_ref, ga_ref, gc_ref, gm_ref,
                  wa_ref, wc_ref, wm_ref, wo_ref, o_ref):
    ya = _dot(on_ref[...], wa_ref[...])
    yc = _dot(yc_ref[...], wc_ref[...])
    ym = _dot(om_ref[...], wm_ref[...])

    merged = (ga_ref[...].astype(F32) * ya + gc_ref[...].astype(F32) * yc
              + gm_ref[...].astype(F32) * ym)
    o_ref[...] = x_ref[...] + _dot(merged.astype(BF16), wo_ref[...])


def _merge(x2d, on2d, proj, wa, wc, wm, wo, tm):
    m, d = x2d.shape
    assert m % tm == 0 and proj.shape == (m, N_COL_BLOCKS * d), (m, tm, proj.shape)

    def col(c):
        return pl.BlockSpec((tm, d), lambda i: (i, c))

    return pl.pallas_call(
        _merge_kernel,
        grid=(m // tm,),
        in_specs=[
            col(0), col(0),
            col(COL_YC), col(COL_OM), col(COL_GATE), col(COL_GATE + 1), col(COL_GATE + 2),
            _resident((d, d)), _resident((d, d)), _resident((d, d)), _resident((d, d)),
        ],
        out_specs=col(0),
        out_shape=jax.ShapeDtypeStruct((m, d), F32),
        compiler_params=pltpu.CompilerParams(
            dimension_semantics=("parallel",), vmem_limit_bytes=VMEM_LIMIT),
        name="merge",
    )(x2d, on2d, proj, proj, proj, proj, proj, wa, wc, wm, wo)


def _mlp_kernel(x_ref, g_ref, w1_ref, w2_ref, o_ref, *, tf):
    x = x_ref[...]
    h = _rms(x, g_ref[...]).astype(BF16)
    acc = x
    for f in range(w1_ref.shape[1] // tf):
        u = jnp.square(jnp.maximum(_dot(h, w1_ref[:, f * tf:(f + 1) * tf]), 0.0))
        acc = acc + _dot(u.astype(BF16), w2_ref[f * tf:(f + 1) * tf, :])
    o_ref[...] = acc


def _mlp(x2d, g, w1, w2, tm, tf):
    m, d = x2d.shape
    assert m % tm == 0 and w1.shape[1] % tf == 0 and w2.shape == w1.shape[::-1], (m, tm, tf)
    return pl.pallas_call(
        functools.partial(_mlp_kernel, tf=tf),
        grid=(m // tm,),
        in_specs=[
            pl.BlockSpec((tm, d), lambda i: (i, 0)),
            _resident((1, d)), _resident(w1.shape), _resident(w2.shape),
        ],
        out_specs=pl.BlockSpec((tm, d), lambda i: (i, 0)),
        out_shape=jax.ShapeDtypeStruct((m, d), F32),
        compiler_params=pltpu.CompilerParams(
            dimension_semantics=("parallel",), vmem_limit_bytes=VMEM_LIMIT),
        name="mlp",
    )(x2d, g, w1, w2)


def _layer(x, mem, lam_init, norm_mix_g, norm_mem_g, w_in, b_gate, q_norm_g, k_norm_g,
           lam_q1, lam_k1, lam_q2, lam_k2, subln_g, w_attn_o, conv_w, w_conv_o, w_mem_kv,
           mq_norm_g, mk_norm_g, w_mem_o, w_o, norm_mlp_g, w_mlp_in, w_mlp_out):
    b, s, d = x.shape
    x2d = x.reshape(b * s, d)
    row = lambda v: v.reshape(1, -1)

    gn = jnp.stack([
        jnp.tile(q_norm_g * (LOG2_E * DIFF_HEAD_DIM ** -0.5), d // DIFF_HEAD_DIM),
        jnp.tile(k_norm_g, d // DIFF_HEAD_DIM),
        jnp.tile(mq_norm_g * MEM_HEAD_DIM ** -0.5, d // MEM_HEAD_DIM),
    ]).reshape(3, 1, d)

    kt, vm, w_in16 = _mem_kv(mem, row(norm_mem_g), w_mem_kv, row(mk_norm_g), w_in)
    later = (w_attn_o, w_conv_o, w_mem_o, w_o, w_mlp_in, w_mlp_out)
    proj, (wa, wc, wm, wo, w1, w2) = _in_proj(
        x2d, row(norm_mix_g), w_in16, gn, conv_w, row(b_gate), kt, vm, later,
        tm=TM_IN_PROJ, seq=s)

    score_bound = DIFF_HEAD_DIM ** 0.5 * jnp.max(jnp.abs(q_norm_g)) * jnp.max(jnp.abs(k_norm_g))
    attn = functools.partial(_diff_attn, proj.reshape(b, s, -1), row(lam_q1), row(lam_k1),
                             row(lam_q2), row(lam_k2), row(subln_g), lam_init)
    o_n = lax.cond(score_bound <= MAX_UNSHIFTED_LOGIT,
                   lambda: attn(tq=TQ_PLAIN, online=False),
                   lambda: attn(tq=TQ_ONLINE, online=True))
    x1 = _merge(x2d, o_n.reshape(b * s, d), proj, wa, wc, wm, wo, tm=TM_MERGE)
    x2 = _mlp(x1, row(norm_mlp_g), w1, w2, tm=TM_MLP, tf=TF_MLP)
    return x2.reshape(b, s, d)


def kernel(x, mem, norm_mix_g, norm_mem_g, w_in, b_gate, q_norm_g, k_norm_g, lam_q1, lam_k1,
           lam_q2, lam_k2, subln_g, w_attn_o, conv_w, w_conv_o, w_mem_kv, mq_norm_g, mk_norm_g,
           w_mem_o, w_o, norm_mlp_g, w_mlp_in, w_mlp_out):
    depth = w_in.shape[0]
    for layer in range(depth):
        lam_init = 0.8 - 0.6 * float(np.exp(-0.3 * layer))
        x = _layer(x, mem, lam_init, norm_mix_g[layer], norm_mem_g[layer], w_in[layer],
                   b_gate[layer], q_norm_g[layer], k_norm_g[layer], lam_q1[layer], lam_k1[layer],
                   lam_q2[layer], lam_k2[layer], subln_g[layer], w_attn_o[layer], conv_w[layer],
                   w_conv_o[layer], w_mem_kv[layer], mq_norm_g[layer], mk_norm_g[layer],
                   w_mem_o[layer], w_o[layer], norm_mlp_g[layer], w_mlp_in[layer],
                   w_mlp_out[layer])
    return x
```

```python
import functools

import jax
import jax.numpy as jnp
import numpy as np
from jax import lax
from jax.experimental import pallas as pl
from jax.experimental.pallas import tpu as pltpu

F32 = jnp.float32
BF16 = jnp.bfloat16

NORM_EPS = 1e-6
MASK_VALUE = -1e30
LOG2_E = 1.4426950408889634
MAX_UNSHIFTED_LOGIT = 60.0

DIFF_HEADS = 8
DIFF_HEAD_DIM = 64
DIFF_V_DIM = 2 * DIFF_HEAD_DIM
MEM_HEADS = 4
MEM_HEAD_DIM = 256
CONV_K = 3

MXU_EDGE = 256
HALO = 8
BF16_SUBLANES = 16

W_Q, W_K, W_V, W_XC, W_GB, W_GC, W_QM, W_GATE = 0, 1, 2, 3, 4, 5, 6, 7
COL_Q, COL_K, COL_V, COL_YC, COL_OM, COL_GATE = 0, 1, 2, 3, 4, 5
N_COL_BLOCKS = 8

VMEM_LIMIT = 56 * 1024 * 1024

TM_IN_PROJ = 512
TM_MERGE = 1024
TM_MLP = 1024
TF_MLP = 1024
TQ_PLAIN = 512
TQ_ONLINE = 512


def _rms(x, g):
    ms = jnp.mean(x * x, axis=-1, keepdims=True)
    return x * lax.rsqrt(ms + NORM_EPS) * g


def _dot(a, b):
    return jnp.dot(a, b, preferred_element_type=F32)


def _dot_nt(a, b):
    return lax.dot_general(a, b, (((1,), (1,)), ((), ())), preferred_element_type=F32)


def _resident(shape):
    return pl.BlockSpec(shape, lambda i: (0,) * len(shape), pipeline_mode=pl.Buffered(1))


def _mem_kv_kernel(mem_ref, g_ref, w_ref, kg_ref, win_ref, kt_ref, v_ref, win16_ref):
    win16_ref[...] = win_ref[...].astype(BF16)
    mem_n = _rms(mem_ref[0], g_ref[...]).astype(BF16)
    kv = _dot(mem_n, w_ref[...].astype(BF16))
    width = kt_ref.shape[1]
    for h in range(MEM_HEADS):
        sl = slice(h * MEM_HEAD_DIM, (h + 1) * MEM_HEAD_DIM)
        kn = _rms(kv[:, sl], kg_ref[...])
        kt_ref[0, sl, :] = kn.T.astype(BF16)
    v_ref[0] = kv[:, width:].astype(BF16)


def _mem_kv(mem, g, w_kv, kg, w_in):
    b, m, d = mem.shape
    width = w_kv.shape[1] // 2
    assert width == MEM_HEADS * MEM_HEAD_DIM and w_in.shape[0] % (BF16_SUBLANES * b) == 0
    w_in_chunk = pl.BlockSpec((w_in.shape[0] // b, w_in.shape[1]), lambda i: (i, 0))
    return pl.pallas_call(
        _mem_kv_kernel,
        grid=(b,),
        in_specs=[
            pl.BlockSpec((1, m, d), lambda i: (i, 0, 0)),
            _resident((1, d)), _resident(w_kv.shape), _resident((1, MEM_HEAD_DIM)),
            w_in_chunk,
        ],
        out_specs=[
            pl.BlockSpec((1, width, m), lambda i: (i, 0, 0)),
            pl.BlockSpec((1, m, width), lambda i: (i, 0, 0)),
            w_in_chunk,
        ],
        out_shape=[
            jax.ShapeDtypeStruct((b, width, m), BF16),
            jax.ShapeDtypeStruct((b, m, width), BF16),
            jax.ShapeDtypeStruct(w_in.shape, BF16),
        ],
        compiler_params=pltpu.CompilerParams(
            dimension_semantics=("parallel",), vmem_limit_bytes=VMEM_LIMIT),
        name="mem_kv",
    )(mem, g, w_kv, kg, w_in)


def _in_proj_kernel(x_ref, g_ref, w_ref, gn_ref, cw_ref, bg_ref, kt_ref, vm_ref, *refs,
                    tiles_per_seq):
    n_cast = (len(refs) - 2) // 2
    o_ref, inner_scr = refs[n_cast], refs[-1]
    for w32_ref, w16_ref in zip(refs[:n_cast], refs[n_cast + 1:-1]):
        w16_ref[...] = w32_ref[...].astype(BF16)
    i = pl.program_id(0)
    tm, d = x_ref.shape

    @pl.when(i % tiles_per_seq == 0)
    def _():
        inner_scr[0:HALO, :] = jnp.zeros((HALO, d), F32)

    h = _rms(x_ref[...], g_ref[...]).astype(BF16)

    def proj(c):
        return _dot(h, w_ref[:, c * d:(c + 1) * d])

    def put(c, val):
        o_ref[:, c * d:(c + 1) * d] = val.astype(BF16)

    def gates():
        for t in range(3):
            z = proj(W_GATE + t) + bg_ref[:, t * d:(t + 1) * d]
            put(COL_GATE + t, 0.5 + 0.5 * jnp.tanh(0.5 * z))

    def conv():
        inner = proj(W_GC) * proj(W_XC)
        inner_scr[HALO:HALO + tm, :] = inner
        taps = (cw_ref[2:3, :] * inner
                + cw_ref[1:2, :] * inner_scr[HALO - 1:HALO - 1 + tm, :]
                + cw_ref[0:1, :] * inner_scr[HALO - 2:HALO - 2 + tm, :])
        put(COL_YC, proj(W_GB) * taps)
        inner_scr[0:HALO, :] = inner_scr[tm:tm + HALO, :]

    def group_normed(slot, src, dst):
        acc = proj(src)
        lane = lax.broadcasted_iota(jnp.int32, (tm, 2 * DIFF_HEAD_DIM), 1)
        low = lane < DIFF_HEAD_DIM
        for c in range(d // (2 * DIFF_HEAD_DIM)):
            sl = slice(c * 2 * DIFF_HEAD_DIM, (c + 1) * 2 * DIFF_HEAD_DIM)
            a = acc[:, sl]
            sq = a * a
            s_lo = jnp.sum(jnp.where(low, sq, 0.0), axis=-1, keepdims=True)
            s_hi = jnp.sum(jnp.where(low, 0.0, sq), axis=-1, keepdims=True)
            ms = jnp.where(low, s_lo, s_hi) * (1.0 / DIFF_HEAD_DIM)
            y = a * lax.rsqrt(ms + NORM_EPS) * gn_ref[slot, :, sl]
            o_ref[:, dst * d + sl.start:dst * d + sl.stop] = y.astype(BF16)

    def memory_attention():
        acc = proj(W_QM)
        for hd in range(MEM_HEADS):
            sl = slice(hd * MEM_HEAD_DIM, (hd + 1) * MEM_HEAD_DIM)
            qn = _rms(acc[:, sl], gn_ref[2, :, sl]).astype(BF16)
            s = _dot(qn, kt_ref[0, sl, :])
            p = jnp.exp(s - jnp.max(s, axis=-1, keepdims=True))
            p = p / jnp.sum(p, axis=-1, keepdims=True)
            o_ref[:, COL_OM * d + sl.start:COL_OM * d + sl.stop] = _dot(
                p.astype(BF16), vm_ref[0, :, sl]).astype(BF16)

    memory_attention()
    group_normed(1, W_K, COL_K)
    group_normed(0, W_Q, COL_Q)
    gates()
    conv()
    put(COL_V, proj(W_V))


def _in_proj(x2d, g, w_in, gn, conv_w, b_gate, kt, vm, later_weights, tm, seq):
    m, d = x2d.shape
    assert seq % tm == 0 and m % seq == 0, (m, seq, tm)
    assert conv_w.shape == (CONV_K, d) and CONV_K - 1 <= HALO, conv_w.shape
    assert w_in.shape == (d, (W_GATE + 3) * d), w_in.shape
    tiles = m // tm
    tiles_per_seq = seq // tm
    assert all(w.shape[0] % (BF16_SUBLANES * tiles) == 0 for w in later_weights)
    chunks = [pl.BlockSpec((w.shape[0] // tiles, w.shape[1]), lambda i: (i, 0))
              for w in later_weights]
    outs = pl.pallas_call(
        functools.partial(_in_proj_kernel, tiles_per_seq=tiles_per_seq),
        grid=(tiles,),
        in_specs=[
            pl.BlockSpec((tm, d), lambda i: (i, 0)),
            _resident((1, d)), _resident(w_in.shape),
            _resident(gn.shape), _resident(conv_w.shape), _resident(b_gate.shape),
            pl.BlockSpec((1,) + kt.shape[1:], lambda i: (i // tiles_per_seq, 0, 0)),
            pl.BlockSpec((1,) + vm.shape[1:], lambda i: (i // tiles_per_seq, 0, 0)),
        ] + chunks,
        out_specs=[pl.BlockSpec((tm, N_COL_BLOCKS * d), lambda i: (i, 0))] + chunks,
        out_shape=[jax.ShapeDtypeStruct((m, N_COL_BLOCKS * d), BF16)]
        + [jax.ShapeDtypeStruct(w.shape, BF16) for w in later_weights],
        scratch_shapes=[pltpu.VMEM((HALO + tm, d), F32)],
        compiler_params=pltpu.CompilerParams(
            dimension_semantics=("arbitrary",), vmem_limit_bytes=VMEM_LIMIT),
        name="in_proj",
    )(x2d, g, w_in, gn, conv_w, b_gate, kt, vm, *later_weights)
    return outs[0], tuple(outs[1:])


def _stack_maps(q):
    lane = lax.broadcasted_iota(jnp.int32, q.shape, 1)
    zero = jnp.zeros_like(q)
    return jnp.concatenate([jnp.where(lane < DIFF_HEAD_DIM, q, zero),
                            jnp.where(lane >= DIFF_HEAD_DIM, q, zero)], axis=0)


def _causal_mask(s, n):
    row = lax.broadcasted_iota(jnp.int32, (n, n), 0)
    col = lax.broadcasted_iota(jnp.int32, (n, n), 1)
    keep = jnp.concatenate([col <= row] * (s.shape[0] // n), axis=0)
    return jnp.where(keep, s, MASK_VALUE)


def _diff_lambda(lq1_ref, lk1_ref, lq2_ref, lk2_ref, lam_init):
    return (jnp.exp(jnp.sum(lq1_ref[...] * lk1_ref[...], axis=-1, keepdims=True))
            - jnp.exp(jnp.sum(lq2_ref[...] * lk2_ref[...], axis=-1, keepdims=True))
            + lam_init)


def _diff_finalize(acc, l, lam, sg, lam_init):
    n = acc.shape[0] // 2
    o = acc[:n] / l[:n] - lam * (acc[n:] / l[n:])
    return (_rms(o, sg) * (1.0 - lam_init)).astype(BF16)


def _diff_attn_online_kernel(lq1_ref, lk1_ref, lq2_ref, lk2_ref, sg_ref, q_ref, k_ref, v_ref,
                             o_ref, *, tq, lam_init):
    i = pl.program_id(2)
    qs = _stack_maps(q_ref[0])

    def step(j, carry, masked):
        m, l, acc = carry
        start = pl.multiple_of(j * tq, tq)
        s = _dot_nt(qs, k_ref[0, pl.ds(start, tq), :])
        if masked:
            s = _causal_mask(s, tq)
        m_new = jnp.maximum(m, jnp.max(s, axis=-1, keepdims=True))
        alpha = jnp.exp2(m - m_new)
        p = jnp.exp2(s - m_new)
        l = alpha * l + jnp.sum(p, axis=-1, keepdims=True)
        acc = alpha * acc + _dot(p.astype(BF16), v_ref[0, pl.ds(start, tq), :])
        return m_new, l, acc

    init = (jnp.full((2 * tq, 1), MASK_VALUE, F32), jnp.zeros((2 * tq, 1), F32),
            jnp.zeros((2 * tq, DIFF_V_DIM), F32))
    carry = lax.fori_loop(0, i, functools.partial(step, masked=False), init)
    _, l, acc = step(i, carry, masked=True)
    lam = _diff_lambda(lq1_ref, lk1_ref, lq2_ref, lk2_ref, lam_init)
    o_ref[0] = _diff_finalize(acc, l, lam, sg_ref[...], lam_init)


def _diff_attn_plain_kernel(lq1_ref, lk1_ref, lq2_ref, lk2_ref, sg_ref, q_ref, k_ref, v_ref,
                            o_ref, *, tq, lam_init, sub=MXU_EDGE):
    seq = q_ref.shape[1]
    lam = _diff_lambda(lq1_ref, lk1_ref, lq2_ref, lk2_ref, lam_init)

    def pv(s, start, size):
        v = v_ref[0, start:start + size, :]
        v_ones = jnp.concatenate([v, jnp.ones_like(v)], axis=1)
        return _dot(jnp.exp2(s).astype(BF16), v_ones)

    for i in reversed(range(seq // tq)):
        r0 = i * tq
        qs = _stack_maps(q_ref[0, r0:r0 + tq, :])
        acc = None
        for j in range(i):
            full = pv(_dot_nt(qs, k_ref[0, j * tq:(j + 1) * tq, :]), j * tq, tq)
            acc = full if acc is None else acc + full
        for c in range(tq // sub):
            n = tq - c * sub
            qs_c = qs if c == 0 else jnp.concatenate([qs[c * sub:tq], qs[tq + c * sub:]], axis=0)
            s = _dot_nt(qs_c, k_ref[0, r0 + c * sub:r0 + (c + 1) * sub, :])
            pieces = [_causal_mask(s[:sub], sub), s[sub:n],
                      _causal_mask(s[n:n + sub], sub), s[n + sub:]]
            s = jnp.concatenate([p for p in pieces if p.shape[0]], axis=0)
            part = pv(s, r0 + c * sub, sub)
            if acc is None:
                acc = part
            elif c == 0:
                acc = acc + part
            else:
                acc = jnp.concatenate([acc[:c * sub], acc[c * sub:tq] + part[:n],
                                       acc[tq:tq + c * sub], acc[tq + c * sub:] + part[n:]],
                                      axis=0)
        o_ref[0, r0:r0 + tq, :] = _diff_finalize(acc[:, :DIFF_V_DIM], acc[:, DIFF_V_DIM:], lam,
                                                 sg_ref[...], lam_init)


def _diff_attn(proj3d, lq1, lk1, lq2, lk2, subln_g, lam_init, tq, online):
    b, s, _ = proj3d.shape
    assert s % tq == 0 and tq % MXU_EDGE == 0, (s, tq)
    rows = tq if online else s
    grid = (b, DIFF_HEADS, s // rows)
    vec = pl.BlockSpec((1, DIFF_HEAD_DIM), lambda bi, h, i: (0, 0))
    body = _diff_attn_online_kernel if online else _diff_attn_plain_kernel
    return pl.pallas_call(
        functools.partial(body, tq=tq, lam_init=lam_init),
        grid=grid,
        in_specs=[
            vec, vec, vec, vec,
            pl.BlockSpec((1, DIFF_V_DIM), lambda bi, h, i: (0, 0)),
            pl.BlockSpec((1, rows, DIFF_V_DIM), lambda bi, h, i: (bi, i, COL_Q * DIFF_HEADS + h)),
            pl.BlockSpec((1, s, DIFF_V_DIM), lambda bi, h, i: (bi, 0, COL_K * DIFF_HEADS + h)),
            pl.BlockSpec((1, s, DIFF_V_DIM), lambda bi, h, i: (bi, 0, COL_V * DIFF_HEADS + h)),
        ],
        out_specs=pl.BlockSpec((1, rows, DIFF_V_DIM), lambda bi, h, i: (bi, i, h)),
        out_shape=jax.ShapeDtypeStruct((b, s, DIFF_HEADS * DIFF_V_DIM), BF16),
        compiler_params=pltpu.CompilerParams(
            dimension_semantics=("parallel", "parallel", "parallel"),
            vmem_limit_bytes=VMEM_LIMIT),
        name="diff_attn_online" if online else "diff_attn",
    )(lq1, lk1, lq2, lk2, subln_g, proj3d, proj3d, proj3d)


def _merge_kernel(x_ref, on_ref, yc_ref, om_ref, ga_ref, gc_ref, gm_ref,
                  wa_ref, wc_ref, wm_ref, wo_ref, o_ref):
    ya = _dot(on_ref[...], wa_ref[...])
    yc = _dot(yc_ref[...], wc_ref[...])
    ym = _dot(om_ref[...], wm_ref[...])

    merged = (ga_ref[...].astype(F32) * ya + gc_ref[...].astype(F32) * yc
              + gm_ref[...].astype(F32) * ym)
    o_ref[...] = x_ref[...] + _dot(merged.astype(BF16), wo_ref[...])


def _merge(x2d, on2d, proj, wa, wc, wm, wo, tm):
    m, d = x2d.shape
    assert m % tm == 0 and proj.shape == (m, N_COL_BLOCKS * d), (m, tm, proj.shape)

    def col(c):
        return pl.BlockSpec((tm, d), lambda i: (i, c))

    return pl.pallas_call(
        _merge_kernel,
        grid=(m // tm,),
        in_specs=[
            col(0), col(0),
            col(COL_YC), col(COL_OM), col(COL_GATE), col(COL_GATE + 1), col(COL_GATE + 2),
            _resident((d, d)), _resident((d, d)), _resident((d, d)), _resident((d, d)),
        ],
        out_specs=col(0),
        out_shape=jax.ShapeDtypeStruct((m, d), F32),
        compiler_params=pltpu.CompilerParams(
            dimension_semantics=("parallel",), vmem_limit_bytes=VMEM_LIMIT),
        name="merge",
    )(x2d, on2d, proj, proj, proj, proj, proj, wa, wc, wm, wo)


def _mlp_kernel(x_ref, g_ref, w1_ref, w2_ref, o_ref, *, tf):
    x = x_ref[...]
    h = _rms(x, g_ref[...]).astype(BF16)
    acc = x
    for f in range(w1_ref.shape[1] // tf):
        u = jnp.square(jnp.maximum(_dot(h, w1_ref[:, f * tf:(f + 1) * tf]), 0.0))
        acc = acc + _dot(u.astype(BF16), w2_ref[f * tf:(f + 1) * tf, :])
    o_ref[...] = acc


def _mlp(x2d, g, w1, w2, tm, tf):
    m, d = x2d.shape
    assert m % tm == 0 and w1.shape[1] % tf == 0 and w2.shape == w1.shape[::-1], (m, tm, tf)
    return pl.pallas_call(
        functools.partial(_mlp_kernel, tf=tf),
        grid=(m // tm,),
        in_specs=[
            pl.BlockSpec((tm, d), lambda i: (i, 0)),
            _resident((1, d)), _resident(w1.shape), _resident(w2.shape),
        ],
        out_specs=pl.BlockSpec((tm, d), lambda i: (i, 0)),
        out_shape=jax.ShapeDtypeStruct((m, d), F32),
        compiler_params=pltpu.CompilerParams(
            dimension_semantics=("parallel",), vmem_limit_bytes=VMEM_LIMIT),
        name="mlp",
    )(x2d, g, w1, w2)


def _layer(x, mem, lam_init, norm_mix_g, norm_mem_g, w_in, b_gate, q_norm_g, k_norm_g,
           lam_q1, lam_k1, lam_q2, lam_k2, subln_g, w_attn_o, conv_w, w_conv_o, w_mem_kv,
           mq_norm_g, mk_norm_g, w_mem_o, w_o, norm_mlp_g, w_mlp_in, w_mlp_out):
    b, s, d = x.shape
    x2d = x.reshape(b * s, d)
    row = lambda v: v.reshape(1, -1)

    gn = jnp.stack([
        jnp.tile(q_norm_g * (LOG2_E * DIFF_HEAD_DIM ** -0.5), d // DIFF_HEAD_DIM),
        jnp.tile(k_norm_g, d // DIFF_HEAD_DIM),
        jnp.tile(mq_norm_g * MEM_HEAD_DIM ** -0.5, d // MEM_HEAD_DIM),
    ]).reshape(3, 1, d)

    kt, vm, w_in16 = _mem_kv(mem, row(norm_mem_g), w_mem_kv, row(mk_norm_g), w_in)
    later = (w_attn_o, w_conv_o, w_mem_o, w_o, w_mlp_in, w_mlp_out)
    proj, (wa, wc, wm, wo, w1, w2) = _in_proj(
        x2d, row(norm_mix_g), w_in16, gn, conv_w, row(b_gate), kt, vm, later,
        tm=TM_IN_PROJ, seq=s)

    score_bound = DIFF_HEAD_DIM ** 0.5 * jnp.max(jnp.abs(q_norm_g)) * jnp.max(jnp.abs(k_norm_g))
    attn = functools.partial(_diff_attn, proj.reshape(b, s, -1), row(lam_q1), row(lam_k1),
                             row(lam_q2), row(lam_k2), row(subln_g), lam_init)
    o_n = lax.cond(score_bound <= MAX_UNSHIFTED_LOGIT,
                   lambda: attn(tq=TQ_PLAIN, online=False),
                   lambda: attn(tq=TQ_ONLINE, online=True))
    x1 = _merge(x2d, o_n.reshape(b * s, d), proj, wa, wc, wm, wo, tm=TM_MERGE)
    x2 = _mlp(x1, row(norm_mlp_g), w1, w2, tm=TM_MLP, tf=TF_MLP)
    return x2.reshape(b, s, d)


def kernel(x, mem, norm_mix_g, norm_mem_g, w_in, b_gate, q_norm_g, k_norm_g, lam_q1, lam_k1,
           lam_q2, lam_k2, subln_g, w_attn_o, conv_w, w_conv_o, w_mem_kv, mq_norm_g, mk_norm_g,
           w_mem_o, w_o, norm_mlp_g, w_mlp_in, w_mlp_out):
    depth = w_in.shape[0]
    for layer in range(depth):
        lam_init = 0.8 - 0.6 * float(np.exp(-0.3 * layer))
        x = _layer(x, mem, lam_init, norm_mix_g[layer], norm_mem_g[layer], w_in[layer],
                   b_gate[layer], q_norm_g[layer], k_norm_g[layer], lam_q1[layer], lam_k1[layer],
                   lam_q2[layer], lam_k2[layer], subln_g[layer], w_attn_o[layer], conv_w[layer],
                   w_conv_o[layer], w_mem_kv[layer], mq_norm_g[layer], mk_norm_g[layer],
                   w_mem_o[layer], w_o[layer], norm_mlp_g[layer], w_mlp_in[layer],
                   w_mlp_out[layer])
    return x
```

```python
import functools

import jax
import jax.numpy as jnp
import numpy as np
from jax import lax
from jax.experimental import pallas as pl
from jax.experimental.pallas import tpu as pltpu

F32 = jnp.float32
BF16 = jnp.bfloat16

NORM_EPS = 1e-6
MASK_VALUE = -1e30
LOG2_E = 1.4426950408889634
MAX_UNSHIFTED_LOGIT = 60.0

DIFF_HEADS = 8
DIFF_HEAD_DIM = 64
DIFF_V_DIM = 2 * DIFF_HEAD_DIM
MEM_HEADS = 4
MEM_HEAD_DIM = 256
CONV_K = 3

MXU_EDGE = 256
HALO = 8
BF16_SUBLANES = 16

W_Q, W_K, W_V, W_XC, W_GB, W_GC, W_QM, W_GATE = 0, 1, 2, 3, 4, 5, 6, 7
COL_Q, COL_K, COL_V, COL_YC, COL_OM, COL_GATE = 0, 1, 2, 3, 4, 5
N_COL_BLOCKS = 8

VMEM_LIMIT = 56 * 1024 * 1024

TM_IN_PROJ = 512
TM_MERGE = 1024
TM_MLP = 1024
TF_MLP = 1024
TM_MERGE_MLP = 512
TQ_PLAIN = 512
TQ_ONLINE = 512


def _rms(x, g):
    ms = jnp.mean(x * x, axis=-1, keepdims=True)
    return x * lax.rsqrt(ms + NORM_EPS) * g


def _dot(a, b):
    return jnp.dot(a, b, preferred_element_type=F32)


def _dot_nt(a, b):
    return lax.dot_general(a, b, (((1,), (1,)), ((), ())), preferred_element_type=F32)


def _resident(shape):
    return pl.BlockSpec(shape, lambda i: (0,) * len(shape), pipeline_mode=pl.Buffered(1))


def _mem_kv_kernel(mem_ref, g_ref, w_ref, kg_ref, win_ref, kt_ref, v_ref, win16_ref):
    win16_ref[...] = win_ref[...].astype(BF16)
    mem_n = _rms(mem_ref[0], g_ref[...]).astype(BF16)
    kv = _dot(mem_n, w_ref[...].astype(BF16))
    width = kt_ref.shape[1]
    for h in range(MEM_HEADS):
        sl = slice(h * MEM_HEAD_DIM, (h + 1) * MEM_HEAD_DIM)
        kn = _rms(kv[:, sl], kg_ref[...])
        kt_ref[0, sl, :] = kn.T.astype(BF16)
    v_ref[0] = kv[:, width:].astype(BF16)


def _mem_kv(mem, g, w_kv, kg, w_in):
    b, m, d = mem.shape
    width = w_kv.shape[1] // 2
    assert width == MEM_HEADS * MEM_HEAD_DIM and w_in.shape[0] % (BF16_SUBLANES * b) == 0
    w_in_chunk = pl.BlockSpec((w_in.shape[0] // b, w_in.shape[1]), lambda i: (i, 0))
    return pl.pallas_call(
        _mem_kv_kernel,
        grid=(b,),
        in_specs=[
            pl.BlockSpec((1, m, d), lambda i: (i, 0, 0)),
            _resident((1, d)), _resident(w_kv.shape), _resident((1, MEM_HEAD_DIM)),
            w_in_chunk,
        ],
        out_specs=[
            pl.BlockSpec((1, width, m), lambda i: (i, 0, 0)),
            pl.BlockSpec((1, m, width), lambda i: (i, 0, 0)),
            w_in_chunk,
        ],
        out_shape=[
            jax.ShapeDtypeStruct((b, width, m), BF16),
            jax.ShapeDtypeStruct((b, m, width), BF16),
            jax.ShapeDtypeStruct(w_in.shape, BF16),
        ],
        compiler_params=pltpu.CompilerParams(
            dimension_semantics=("parallel",), vmem_limit_bytes=VMEM_LIMIT),
        name="mem_kv",
    )(mem, g, w_kv, kg, w_in)


def _in_proj_kernel(x_ref, g_ref, w_ref, gn_ref, cw_ref, bg_ref, kt_ref, vm_ref, *refs,
                    tiles_per_seq):
    n_cast = (len(refs) - 2) // 2
    o_ref, inner_scr = refs[n_cast], refs[-1]
    for w32_ref, w16_ref in zip(refs[:n_cast], refs[n_cast + 1:-1]):
        w16_ref[...] = w32_ref[...].astype(BF16)
    i = pl.program_id(0)
    tm, d = x_ref.shape

    @pl.when(i % tiles_per_seq == 0)
    def _():
        inner_scr[0:HALO, :] = jnp.zeros((HALO, d), F32)

    h = _rms(x_ref[...], g_ref[...]).astype(BF16)

    def proj(c):
        return _dot(h, w_ref[:, c * d:(c + 1) * d])

    def put(c, val):
        o_ref[:, c * d:(c + 1) * d] = val.astype(BF16)

    def gates():
        for t in range(3):
            z = proj(W_GATE + t) + bg_ref[:, t * d:(t + 1) * d]
            put(COL_GATE + t, 0.5 + 0.5 * jnp.tanh(0.5 * z))

    def conv():
        inner = proj(W_GC) * proj(W_XC)
        inner_scr[HALO:HALO + tm, :] = inner
        taps = (cw_ref[2:3, :] * inner
                + cw_ref[1:2, :] * inner_scr[HALO - 1:HALO - 1 + tm, :]
                + cw_ref[0:1, :] * inner_scr[HALO - 2:HALO - 2 + tm, :])
        put(COL_YC, proj(W_GB) * taps)
        inner_scr[0:HALO, :] = inner_scr[tm:tm + HALO, :]

    def group_normed(slot, src, dst):
        acc = proj(src)
        lane = lax.broadcasted_iota(jnp.int32, (tm, 2 * DIFF_HEAD_DIM), 1)
        low = lane < DIFF_HEAD_DIM
        for c in range(d // (2 * DIFF_HEAD_DIM)):
            sl = slice(c * 2 * DIFF_HEAD_DIM, (c + 1) * 2 * DIFF_HEAD_DIM)
            a = acc[:, sl]
            sq = a * a
            s_lo = jnp.sum(jnp.where(low, sq, 0.0), axis=-1, keepdims=True)
            s_hi = jnp.sum(jnp.where(low, 0.0, sq), axis=-1, keepdims=True)
            ms = jnp.where(low, s_lo, s_hi) * (1.0 / DIFF_HEAD_DIM)
            y = a * lax.rsqrt(ms + NORM_EPS) * gn_ref[slot, :, sl]
            o_ref[:, dst * d + sl.start:dst * d + sl.stop] = y.astype(BF16)

    def memory_attention():
        acc = proj(W_QM)
        for hd in range(MEM_HEADS):
            sl = slice(hd * MEM_HEAD_DIM, (hd + 1) * MEM_HEAD_DIM)
            qn = _rms(acc[:, sl], gn_ref[2, :, sl]).astype(BF16)
            s = _dot(qn, kt_ref[0, sl, :])
            p = jnp.exp(s - jnp.max(s, axis=-1, keepdims=True))
            p = p / jnp.sum(p, axis=-1, keepdims=True)
            o_ref[:, COL_OM * d + sl.start:COL_OM * d + sl.stop] = _dot(
                p.astype(BF16), vm_ref[0, :, sl]).astype(BF16)

    memory_attention()
    group_normed(1, W_K, COL_K)
    group_normed(0, W_Q, COL_Q)
    gates()
    conv()
    put(COL_V, proj(W_V))


def _in_proj(x2d, g, w_in, gn, conv_w, b_gate, kt, vm, later_weights, tm, seq):
    m, d = x2d.shape
    assert seq % tm == 0 and m % seq == 0, (m, seq, tm)
    assert conv_w.shape == (CONV_K, d) and CONV_K - 1 <= HALO, conv_w.shape
    assert w_in.shape == (d, (W_GATE + 3) * d), w_in.shape
    tiles = m // tm
    tiles_per_seq = seq // tm
    assert all(w.shape[0] % (BF16_SUBLANES * tiles) == 0 for w in later_weights)
    chunks = [pl.BlockSpec((w.shape[0] // tiles, w.shape[1]), lambda i: (i, 0))
              for w in later_weights]
    outs = pl.pallas_call(
        functools.partial(_in_proj_kernel, tiles_per_seq=tiles_per_seq),
        grid=(tiles,),
        in_specs=[
            pl.BlockSpec((tm, d), lambda i: (i, 0)),
            _resident((1, d)), _resident(w_in.shape),
            _resident(gn.shape), _resident(conv_w.shape), _resident(b_gate.shape),
            pl.BlockSpec((1,) + kt.shape[1:], lambda i: (i // tiles_per_seq, 0, 0)),
            pl.BlockSpec((1,) + vm.shape[1:], lambda i: (i // tiles_per_seq, 0, 0)),
        ] + chunks,
        out_specs=[pl.BlockSpec((tm, N_COL_BLOCKS * d), lambda i: (i, 0))] + chunks,
        out_shape=[jax.ShapeDtypeStruct((m, N_COL_BLOCKS * d), BF16)]
        + [jax.ShapeDtypeStruct(w.shape, BF16) for w in later_weights],
        scratch_shapes=[pltpu.VMEM((HALO + tm, d), F32)],
        compiler_params=pltpu.CompilerParams(
            dimension_semantics=("arbitrary",), vmem_limit_bytes=VMEM_LIMIT),
        name="in_proj",
    )(x2d, g, w_in, gn, conv_w, b_gate, kt, vm, *later_weights)
    return outs[0], tuple(outs[1:])


def _stack_maps(q):
    lane = lax.broadcasted_iota(jnp.int32, q.shape, 1)
    zero = jnp.zeros_like(q)
    return jnp.concatenate([jnp.where(lane < DIFF_HEAD_DIM, q, zero),
                            jnp.where(lane >= DIFF_HEAD_DIM, q, zero)], axis=0)


def _causal_mask(s, n):
    row = lax.broadcasted_iota(jnp.int32, (n, n), 0)
    col = lax.broadcasted_iota(jnp.int32, (n, n), 1)
    keep = jnp.concatenate([col <= row] * (s.shape[0] // n), axis=0)
    return jnp.where(keep, s, MASK_VALUE)


def _diff_lambda(lq1_ref, lk1_ref, lq2_ref, lk2_ref, lam_init):
    return (jnp.exp(jnp.sum(lq1_ref[...] * lk1_ref[...], axis=-1, keepdims=True))
            - jnp.exp(jnp.sum(lq2_ref[...] * lk2_ref[...], axis=-1, keepdims=True))
            + lam_init)


def _diff_finalize(acc, l, lam, sg, lam_init):
    n = acc.shape[0] // 2
    o = acc[:n] / l[:n] - lam * (acc[n:] / l[n:])
    return (_rms(o, sg) * (1.0 - lam_init)).astype(BF16)


def _diff_attn_online_kernel(lq1_ref, lk1_ref, lq2_ref, lk2_ref, sg_ref, q_ref, k_ref, v_ref,
                             o_ref, *, tq, lam_init):
    i = pl.program_id(2)
    qs = _stack_maps(q_ref[0])

    def step(j, carry, masked):
        m, l, acc = carry
        start = pl.multiple_of(j * tq, tq)
        s = _dot_nt(qs, k_ref[0, pl.ds(start, tq), :])
        if masked:
            s = _causal_mask(s, tq)
        m_new = jnp.maximum(m, jnp.max(s, axis=-1, keepdims=True))
        alpha = jnp.exp2(m - m_new)
        p = jnp.exp2(s - m_new)
        l = alpha * l + jnp.sum(p, axis=-1, keepdims=True)
        acc = alpha * acc + _dot(p.astype(BF16), v_ref[0, pl.ds(start, tq), :])
        return m_new, l, acc

    init = (jnp.full((2 * tq, 1), MASK_VALUE, F32), jnp.zeros((2 * tq, 1), F32),
            jnp.zeros((2 * tq, DIFF_V_DIM), F32))
    carry = lax.fori_loop(0, i, functools.partial(step, masked=False), init)
    _, l, acc = step(i, carry, masked=True)
    lam = _diff_lambda(lq1_ref, lk1_ref, lq2_ref, lk2_ref, lam_init)
    o_ref[0] = _diff_finalize(acc, l, lam, sg_ref[...], lam_init)


def _diff_attn_plain_kernel(lq1_ref, lk1_ref, lq2_ref, lk2_ref, sg_ref, q_ref, k_ref, v_ref,
                            o_ref, *, tq, lam_init, sub=MXU_EDGE):
    seq = q_ref.shape[1]
    lam = _diff_lambda(lq1_ref, lk1_ref, lq2_ref, lk2_ref, lam_init)

    def pv(s, start, size):
        v = v_ref[0, start:start + size, :]
        v_ones = jnp.concatenate([v, jnp.ones_like(v)], axis=1)
        return _dot(jnp.exp2(s).astype(BF16), v_ones)

    for i in reversed(range(seq // tq)):
        r0 = i * tq
        qs = _stack_maps(q_ref[0, r0:r0 + tq, :])
        acc = None
        for j in range(i):
            full = pv(_dot_nt(qs, k_ref[0, j * tq:(j + 1) * tq, :]), j * tq, tq)
            acc = full if acc is None else acc + full
        for c in range(tq // sub):
            n = tq - c * sub
            qs_c = qs if c == 0 else jnp.concatenate([qs[c * sub:tq], qs[tq + c * sub:]], axis=0)
            s = _dot_nt(qs_c, k_ref[0, r0 + c * sub:r0 + (c + 1) * sub, :])
            pieces = [_causal_mask(s[:sub], sub), s[sub:n],
                      _causal_mask(s[n:n + sub], sub), s[n + sub:]]
            s = jnp.concatenate([p for p in pieces if p.shape[0]], axis=0)
            part = pv(s, r0 + c * sub, sub)
            if acc is None:
                acc = part
            elif c == 0:
                acc = acc + part
            else:
                acc = jnp.concatenate([acc[:c * sub], acc[c * sub:tq] + part[:n],
                                       acc[tq:tq + c * sub], acc[tq + c * sub:] + part[n:]],
                                      axis=0)
        o_ref[0, r0:r0 + tq, :] = _diff_finalize(acc[:, :DIFF_V_DIM], acc[:, DIFF_V_DIM:], lam,
                                                 sg_ref[...], lam_init)


def _diff_attn(proj3d, lq1, lk1, lq2, lk2, subln_g, lam_init, tq, online):
    b, s, _ = proj3d.shape
    assert s % tq == 0 and tq % MXU_EDGE == 0, (s, tq)
    rows = tq if online else s
    grid = (b, DIFF_HEADS, s // rows)
    vec = pl.BlockSpec((1, DIFF_HEAD_DIM), lambda bi, h, i: (0, 0))
    body = _diff_attn_online_kernel if online else _diff_attn_plain_kernel
    return pl.pallas_call(
        functools.partial(body, tq=tq, lam_init=lam_init),
        grid=grid,
        in_specs=[
            vec, vec, vec, vec,
            pl.BlockSpec((1, DIFF_V_DIM), lambda bi, h, i: (0, 0)),
            pl.BlockSpec((1, rows, DIFF_V_DIM), lambda bi, h, i: (bi, i, COL_Q * DIFF_HEADS + h)),
            pl.BlockSpec((1, s, DIFF_V_DIM), lambda bi, h, i: (bi, 0, COL_K * DIFF_HEADS + h)),
            pl.BlockSpec((1, s, DIFF_V_DIM), lambda bi, h, i: (bi, 0, COL_V * DIFF_HEADS + h)),
        ],
        out_specs=pl.BlockSpec((1, rows, DIFF_V_DIM), lambda bi, h, i: (bi, i, h)),
        out_shape=jax.ShapeDtypeStruct((b, s, DIFF_HEADS * DIFF_V_DIM), BF16),
        compiler_params=pltpu.CompilerParams(
            dimension_semantics=("parallel", "parallel", "parallel"),
            vmem_limit_bytes=VMEM_LIMIT),
        name="diff_attn_online" if online else "diff_attn",
    )(lq1, lk1, lq2, lk2, subln_g, proj3d, proj3d, proj3d)


def _merge_kernel(x_ref, on_ref, yc_ref, om_ref, ga_ref, gc_ref, gm_ref,
                  wa_ref, wc_ref, wm_ref, wo_ref, o_ref):
    ya = _dot(on_ref[...], wa_ref[...])
    yc = _dot(yc_ref[...], wc_ref[...])
    ym = _dot(om_ref[...], wm_ref[...])

    merged = (ga_ref[...].astype(F32) * ya + gc_ref[...].astype(F32) * yc
              + gm_ref[...].astype(F32) * ym)
    o_ref[...] = x_ref[...] + _dot(merged.astype(BF16), wo_ref[...])


def _merge(x2d, on2d, proj, wa, wc, wm, wo, tm):
    m, d = x2d.shape
    assert m % tm == 0 and proj.shape == (m, N_COL_BLOCKS * d), (m, tm, proj.shape)

    def col(c):
        return pl.BlockSpec((tm, d), lambda i: (i, c))

    return pl.pallas_call(
        _merge_kernel,
        grid=(m // tm,),
        in_specs=[
            col(0), col(0),
            col(COL_YC), col(COL_OM), col(COL_GATE), col(COL_GATE + 1), col(COL_GATE + 2),
            _resident((d, d)), _resident((d, d)), _resident((d, d)), _resident((d, d)),
        ],
        out_specs=col(0),
        out_shape=jax.ShapeDtypeStruct((m, d), F32),
        compiler_params=pltpu.CompilerParams(
            dimension_semantics=("parallel",), vmem_limit_bytes=VMEM_LIMIT),
        name="merge",
    )(x2d, on2d, proj, proj, proj, proj, proj, wa, wc, wm, wo)


def _mlp_kernel(x_ref, g_ref, w1_ref, w2_ref, o_ref, *, tf):
    x = x_ref[...]
    h = _rms(x, g_ref[...]).astype(BF16)
    acc = x
    for f in range(w1_ref.shape[1] // tf):
        u = jnp.square(jnp.maximum(_dot(h, w1_ref[:, f * tf:(f + 1) * tf]), 0.0))
        acc = acc + _dot(u.astype(BF16), w2_ref[f * tf:(f + 1) * tf, :])
    o_ref[...] = acc


def _mlp(x2d, g, w1, w2, tm, tf):
    m, d = x2d.shape
    assert m % tm == 0 and w1.shape[1] % tf == 0 and w2.shape == w1.shape[::-1], (m, tm, tf)
    return pl.pallas_call(
        functools.partial(_mlp_kernel, tf=tf),
        grid=(m // tm,),
        in_specs=[
            pl.BlockSpec((tm, d), lambda i: (i, 0)),
            _resident((1, d)), _resident(w1.shape), _resident(w2.shape),
        ],
        out_specs=pl.BlockSpec((tm, d), lambda i: (i, 0)),
        out_shape=jax.ShapeDtypeStruct((m, d), F32),
        compiler_params=pltpu.CompilerParams(
            dimension_semantics=("parallel",), vmem_limit_bytes=VMEM_LIMIT),
        name="mlp",
    )(x2d, g, w1, w2)


def _merge_mlp_kernel(x_ref, on_ref, yc_ref, om_ref, ga_ref, gc_ref, gm_ref,
                      wa_ref, wc_ref, wm_ref, wo_ref, g_ref, w1_ref, w2_ref, o_ref, *, tf):
    ya = _dot(on_ref[...], wa_ref[...])
    yc = _dot(yc_ref[...], wc_ref[...])
    ym = _dot(om_ref[...], wm_ref[...])
    merged = (ga_ref[...].astype(F32) * ya + gc_ref[...].astype(F32) * yc
              + gm_ref[...].astype(F32) * ym)
    x1 = x_ref[...] + _dot(merged.astype(BF16), wo_ref[...])
    h = _rms(x1, g_ref[...]).astype(BF16)
    acc = x1
    for f in range(w1_ref.shape[1] // tf):
        u = jnp.square(jnp.maximum(_dot(h, w1_ref[:, f * tf:(f + 1) * tf]), 0.0))
        acc = acc + _dot(u.astype(BF16), w2_ref[f * tf:(f + 1) * tf, :])
    o_ref[...] = acc


def _merge_mlp(x2d, on2d, proj, wa, wc, wm, wo, g, w1, w2, tm, tf):
    m, d = x2d.shape
    assert m % tm == 0 and proj.shape == (m, N_COL_BLOCKS * d), (m, tm, proj.shape)
    assert w1.shape[1] % tf == 0 and w2.shape == w1.shape[::-1]

    def col(c):
        return pl.BlockSpec((tm, d), lambda i: (i, c))

    return pl.pallas_call(
        functools.partial(_merge_mlp_kernel, tf=tf),
        grid=(m // tm,),
        in_specs=[
            col(0), col(0),
            col(COL_YC), col(COL_OM), col(COL_GATE), col(COL_GATE + 1), col(COL_GATE + 2),
            _resident((d, d)), _resident((d, d)), _resident((d, d)), _resident((d, d)),
            _resident((1, d)), _resident(w1.shape), _resident(w2.shape),
        ],
        out_specs=col(0),
        out_shape=jax.ShapeDtypeStruct((m, d), F32),
        compiler_params=pltpu.CompilerParams(
            dimension_semantics=("parallel",), vmem_limit_bytes=VMEM_LIMIT),
        name="merge_mlp",
    )(x2d, on2d, proj, proj, proj, proj, proj, wa, wc, wm, wo, g, w1, w2)


def _layer(x, mem, lam_init, norm_mix_g, norm_mem_g, w_in, b_gate, q_norm_g, k_norm_g,
           lam_q1, lam_k1, lam_q2, lam_k2, subln_g, w_attn_o, conv_w, w_conv_o, w_mem_kv,
           mq_norm_g, mk_norm_g, w_mem_o, w_o, norm_mlp_g, w_mlp_in, w_mlp_out):
    b, s, d = x.shape
    x2d = x.reshape(b * s, d)
    row = lambda v: v.reshape(1, -1)

    gn = jnp.stack([
        jnp.tile(q_norm_g * (LOG2_E * DIFF_HEAD_DIM ** -0.5), d // DIFF_HEAD_DIM),
        jnp.tile(k_norm_g, d // DIFF_HEAD_DIM),
        jnp.tile(mq_norm_g * MEM_HEAD_DIM ** -0.5, d // MEM_HEAD_DIM),
    ]).reshape(3, 1, d)

    kt, vm, w_in16 = _mem_kv(mem, row(norm_mem_g), w_mem_kv, row(mk_norm_g), w_in)
    later = (w_attn_o, w_conv_o, w_mem_o, w_o, w_mlp_in, w_mlp_out)
    proj, (wa, wc, wm, wo, w1, w2) = _in_proj(
        x2d, row(norm_mix_g), w_in16, gn, conv_w, row(b_gate), kt, vm, later,
        tm=TM_IN_PROJ, seq=s)

    score_bound = DIFF_HEAD_DIM ** 0.5 * jnp.max(jnp.abs(q_norm_g)) * jnp.max(jnp.abs(k_norm_g))
    attn = functools.partial(_diff_attn, proj.reshape(b, s, -1), row(lam_q1), row(lam_k1),
                             row(lam_q2), row(lam_k2), row(subln_g), lam_init)
    o_n = lax.cond(score_bound <= MAX_UNSHIFTED_LOGIT,
                   lambda: attn(tq=TQ_PLAIN, online=False),
                   lambda: attn(tq=TQ_ONLINE, online=True))
    x2 = _merge_mlp(x2d, o_n.reshape(b * s, d), proj, wa, wc, wm, wo, row(norm_mlp_g), w1, w2,
                    tm=TM_MERGE_MLP, tf=TF_MLP)
    return x2.reshape(b, s, d)


def kernel(x, mem, norm_mix_g, norm_mem_g, w_in, b_gate, q_norm_g, k_norm_g, lam_q1, lam_k1,
           lam_q2, lam_k2, subln_g, w_attn_o, conv_w, w_conv_o, w_mem_kv, mq_norm_g, mk_norm_g,
           w_mem_o, w_o, norm_mlp_g, w_mlp_in, w_mlp_out):
    depth = w_in.shape[0]
    for layer in range(depth):
        lam_init = 0.8 - 0.6 * float(np.exp(-0.3 * layer))
        x = _layer(x, mem, lam_init, norm_mix_g[layer], norm_mem_g[layer], w_in[layer],
                   b_gate[layer], q_norm_g[layer], k_norm_g[layer], lam_q1[layer], lam_k1[layer],
                   lam_q2[layer], lam_k2[layer], subln_g[layer], w_attn_o[layer], conv_w[layer],
                   w_conv_o[layer], w_mem_kv[layer], mq_norm_g[layer], mk_norm_g[layer],
                   w_mem_o[layer], w_o[layer], norm_mlp_g[layer], w_mlp_in[layer],
                   w_mlp_out[layer])
    return x
```
